```python
import math
import jax
import jax.numpy as jnp
from jax import lax
import numpy as np

D_MODEL = 2048
BATCH = 8
SEQ = 4096
DEPTH = 4

CTX_LEN = 256
GRID_W = 64
F32 = jnp.float32
EPS = 1e-6
CHUNK = 64

N_BRANCH = 4
BR_W = D_MODEL // N_BRANCH

HY_W = BR_W
HY_ORDER = 2
HY_SHORT = 3
HY_EMB = 33
HY_BANDS = (HY_EMB - 1) // 2
HY_HID = 64
HY_INNER = 2
HY_FAST_DECAY = 0.3
HY_SLOW_DECAY = 1.5
HY_TARGET = 1e-2
HY_MIN_DECAY = math.log(HY_TARGET) / HY_SLOW_DECAY
HY_MAX_DECAY = math.log(HY_TARGET) / HY_FAST_DECAY

GDN_H = 4
GDN_DK = BR_W // GDN_H
GDN_DV = BR_W // GDN_H
GDN_W = GDN_H * GDN_DV
GDN_SHORT = 3

RET_H = 4
RET_DK = BR_W // (2 * RET_H)
RET_DV = BR_W // RET_H
RET_W = RET_H * RET_DV
RET_ROPE_PAIRS = (8, 12, 12)
ROPE_BASE = 10000.0

GLA_H = 4
GLA_DK = BR_W // (2 * GLA_H)
GLA_DV = BR_W // GLA_H
GLA_W = GLA_H * GLA_DV
GLA_RANK = 16
GLA_TAU = 16.0

IN_SPLITS = (
    ('hy_proj', 3 * HY_W), ('hy_gate', HY_W),
    ('gdn_qkv', 3 * GDN_W), ('gdn_a', 2 * GDN_H), ('gdn_b', 2 * GDN_H), ('gdn_gate', GDN_W),
    ('ret_qk', 2 * RET_H * RET_DK), ('ret_v', RET_W), ('ret_gate', RET_W),
    ('gla_qk', 2 * GLA_H * GLA_DK), ('gla_v', GLA_W), ('gla_lr', 2 * GLA_RANK), ('gla_gate', GLA_W),
)
N_IN = sum(size for _, size in IN_SPLITS)

kernel_name = 'hybrid_hyena_gdn_retnet_gla_dit'


def _rmsnorm(x, w):
    x32 = x.astype(F32)
    y = x32 * lax.rsqrt(jnp.mean(x32 * x32, axis=-1, keepdims=True) + EPS) * w.astype(F32)
    return y.astype(x.dtype)


def _head_norm(o, w):
    o = o * lax.rsqrt(jnp.mean(o * o, axis=-1, keepdims=True) + EPS) * w.astype(F32)
    B, H, L, d = o.shape
    return jnp.transpose(o, (0, 2, 1, 3)).reshape(B, L, H * d)


def _l2norm(t):
    return t * lax.rsqrt(jnp.sum(t * t, axis=-1, keepdims=True) + EPS)


def _heads(t, H):
    B, L, W = t.shape
    return jnp.transpose(t.reshape(B, L, H, W // H), (0, 2, 1, 3))


def _split_in(u):
    parts = {}
    off = 0
    for name, size in IN_SPLITS:
        parts[name] = u[..., off:off + size]
        off += size
    return parts


def _short_conv(x, w, b=None):
    K, C = w.shape
    p = K // 2
    y = lax.conv_general_dilated(x, w[:, None, :].astype(x.dtype), (1,), [(p, p)],
                                 dimension_numbers=('NWC', 'WIO', 'NWC'), feature_group_count=C)
    return y if b is None else y + b


def _causal_mask(strict=False):
    return jnp.tril(jnp.ones((CHUNK, CHUNK), bool), -1 if strict else 0)


def _to_chunks(t):
    return t.reshape(t.shape[0], t.shape[1], t.shape[2] // CHUNK, CHUNK, *t.shape[3:])


def _chunks_first(*ts):
    return tuple(jnp.moveaxis(t, 2, 0) for t in ts)


def _scalar_decay(G):
    return jnp.exp(jnp.where(_causal_mask(), G[..., :, None] - G[..., None, :], -jnp.inf))


def _retention_chunk_scan(q, k, v, g, s0):
    qc, kc, vc = _to_chunks(q), _to_chunks(k), _to_chunks(v)
    G = jnp.cumsum(_to_chunks(g), axis=-1)
    scores = jnp.einsum('bhnid,bhnjd->bhnij', qc, kc) * _scalar_decay(G)
    intra = jnp.einsum('bhnij,bhnjv->bhniv', scores, vc)
    q_in = qc * jnp.exp(G)[..., None]
    k_out = kc * jnp.exp(G[..., -1:] - G)[..., None]
    a_end = jnp.exp(G[..., -1])

    def step(s, xs):
        qi, ki, vi, ai = xs
        o = jnp.einsum('bhid,bhdv->bhiv', qi, s)
        s = s * ai[..., None, None] + jnp.einsum('bhid,bhiv->bhdv', ki, vi)
        return s, o

    s, inter = lax.scan(step, s0, _chunks_first(q_in, k_out, vc, a_end))
    o = intra + jnp.moveaxis(inter, 0, 2)
    return o.reshape(o.shape[0], o.shape[1], -1, o.shape[-1]), s


def _gated_delta_chunk_scan(q, k, v, beta, g, s0):
    qc, kc, vc, bc = _to_chunks(q), _to_chunks(k), _to_chunks(v), _to_chunks(beta)
    G = jnp.cumsum(_to_chunks(g), axis=-1)
    decay = _scalar_decay(G)
    kb = kc * bc[..., None]
    lmat = jnp.where(_causal_mask(strict=True),
                     jnp.einsum('bhnid,bhnjd->bhnij', kb, kc) * decay, 0.0)
    rhs = jnp.concatenate([vc * bc[..., None], kb * jnp.exp(G)[..., None]], axis=-1)
    sol = lax.linalg.triangular_solve(lmat, rhs, left_side=True, lower=True, unit_diagonal=True)
    dv = vc.shape[-1]
    u, w = sol[..., :dv], sol[..., dv:]
    scores = jnp.einsum('bhnid,bhnjd->bhnij', qc, kc) * decay
    q_in = qc * jnp.exp(G)[..., None]
    k_out = kc * jnp.exp(G[..., -1:] - G)[..., None]
    a_end = jnp.exp(G[..., -1])

    def step(s, xs):
        ui, wi, si, qi, ki, ai = xs
        v_new = ui - jnp.einsum('bhid,bhdv->bhiv', wi, s)
        o = jnp.einsum('bhid,bhdv->bhiv', qi, s) + jnp.einsum('bhij,bhjv->bhiv', si, v_new)
        s = s * ai[..., None, None] + jnp.einsum('bhid,bhiv->bhdv', ki, v_new)
        return s, o

    s, o = lax.scan(step, s0, _chunks_first(u, w, scores, q_in, k_out, a_end))
    o = jnp.moveaxis(o, 0, 2)
    return o.reshape(o.shape[0], o.shape[1], -1, o.shape[-1]), s


def _gla_chunk_scan(q, k, v, gk, s0):
    qc, kc, vc = _to_chunks(q), _to_chunks(k), _to_chunks(v)
    G = jnp.cumsum(_to_chunks(gk), axis=3)
    mask = _causal_mask()[:, :, None]

    def step(s, xs):
        qi, ki, vi, Gi = xs
        rel = jnp.exp(jnp.where(mask, Gi[:, :, :, None, :] - Gi[:, :, None, :, :], -jnp.inf))
        scores = jnp.einsum('bhid,bhjd,bhijd->bhij', qi, ki, rel)
        g_end = Gi[:, :, -1:, :]
        o = (jnp.einsum('bhid,bhdv->bhiv', qi * jnp.exp(Gi), s)
             + jnp.einsum('bhij,bhjv->bhiv', scores, vi))
        s = (s * jnp.exp(g_end)[:, :, 0, :, None]
             + jnp.einsum('bhid,bhiv->bhdv', ki * jnp.exp(g_end - Gi), vi))
        return s, o

    s, o = lax.scan(step, s0, _chunks_first(qc, kc, vc, G))
    o = jnp.moveaxis(o, 0, 2)
    return o.reshape(o.shape[0], o.shape[1], -1, o.shape[-1]), s


def _bidirectional(scan_fn, ctx_f, ctx_b, lat_f, lat_b, s0):
    rev = lambda ts: tuple(jnp.flip(t, axis=2) for t in ts)
    oc_f, sc_f = scan_fn(*ctx_f, s0)
    oc_b, sc_b = scan_fn(*rev(ctx_b), s0)
    ol_f, _ = scan_fn(*lat_f, sc_f)
    ol_b, _ = scan_fn(*rev(lat_b), sc_b)
    return oc_f + jnp.flip(oc_b, axis=2), ol_f + jnp.flip(ol_b, axis=2)


def _hyena_filters(L, w1, b1, w2, b2, w3, freq):
    t = jnp.linspace(0.0, 1.0, L, dtype=F32)[:, None]
    w = 2.0 * math.pi * jnp.arange(L, dtype=F32) / L
    f = jnp.linspace(1e-4, HY_BANDS - 1, HY_BANDS, dtype=F32)
    ang = w[:, None] * f[None, :]
    z = jnp.concatenate([t, jnp.cos(ang), -jnp.sin(ang)], axis=-1)
    hdn = jnp.sin(freq[0].astype(F32) * (z @ w1.astype(F32) + b1.astype(F32)))
    for i in range(HY_INNER):
        hdn = jnp.sin(freq[i + 1].astype(F32) * (hdn @ w2[i].astype(F32) + b2[i].astype(F32)))
    h = (hdn @ w3.astype(F32)).reshape(L, HY_ORDER, 2, HY_W)
    deltas = jnp.abs(jnp.linspace(HY_MIN_DECAY, HY_MAX_DECAY, HY_W, dtype=F32))
    h = h * jnp.exp(-t[:, :, None, None] * deltas)
    hf, hb = h[:, :, 0], h[:, :, 1]
    k2 = jnp.concatenate([hf, jnp.zeros_like(hf[:1]), jnp.flip(hb[1:], axis=0)], axis=0)
    k2 = k2 * lax.rsqrt(jnp.sum(k2 * k2, axis=0, keepdims=True) + EPS)
    return jnp.fft.rfft(k2, axis=0)


def _fft_long_conv(z, kf):
    L = z.shape[1]
    zf = jnp.fft.rfft(z, n=2 * L, axis=1)
    return jnp.fft.irfft(zf * kf[None], n=2 * L, axis=1)[:, :L]


def _hyena_mixer(z, conv_w, conv_b, w1, b1, w2, b2, w3, freq, skip):
    L = z.shape[1]
    z = _short_conv(z, conv_w, conv_b).astype(F32)
    v, x1, x2 = jnp.split(z, 3, axis=-1)
    kf = _hyena_filters(L, w1, b1, w2, b2, w3, freq)
    skip = skip.astype(F32)
    y = x1 * (_fft_long_conv(v, kf[:, 0]) + skip[0] * v)
    y = x2 * (_fft_long_conv(y, kf[:, 1]) + skip[1] * y)
    return y


def _gdn_stream(p, conv_w, a_log, dt_bias):
    B, L, _ = p['gdn_qkv'].shape
    qkv = jax.nn.silu(_short_conv(p['gdn_qkv'], conv_w)).astype(F32)
    q, k, v = jnp.split(qkv, 3, axis=-1)
    q = _l2norm(_heads(q, GDN_H)) * GDN_DK ** -0.5
    k = _l2norm(_heads(k, GDN_H))
    v = _heads(v, GDN_H)
    a = p['gdn_a'].astype(F32).reshape(B, L, 2, GDN_H)
    b = p['gdn_b'].astype(F32).reshape(B, L, 2, GDN_H)
    g = -jnp.exp(a_log.astype(F32)) * jax.nn.softplus(a + dt_bias.astype(F32))
    g = jnp.transpose(g, (2, 0, 3, 1))
    beta = jnp.transpose(jax.nn.sigmoid(b), (2, 0, 3, 1))
    return (q, k, v, beta[0], g[0]), (q, k, v, beta[1], g[1])


def _gdn_mixer(p_ctx, p_lat, conv_w, a_log, dt_bias, norm_w):
    cf, cb = _gdn_stream(p_ctx, conv_w, a_log, dt_bias)
    lf, lb = _gdn_stream(p_lat, conv_w, a_log, dt_bias)
    s0 = jnp.zeros((cf[0].shape[0], GDN_H, GDN_DK, GDN_DV), F32)
    oc, ol = _bidirectional(_gated_delta_chunk_scan, cf, cb, lf, lb, s0)
    return _head_norm(oc, norm_w), _head_norm(ol, norm_w)


def _ctx_positions(L):
    t = jnp.arange(L, dtype=F32)
    zero = jnp.zeros((L,), F32)
    return jnp.stack([t, zero, zero], axis=-1)


def _latent_positions(L, offset):
    rows = L // GRID_W
    r = jnp.repeat(jnp.arange(rows, dtype=F32), GRID_W)
    col = jnp.broadcast_to(jnp.arange(GRID_W, dtype=F32), (rows, GRID_W)).reshape(-1)
    return jnp.stack([jnp.full((L,), offset, F32), r, col], axis=-1)


def _rope(t, pos):
    angs = []
    for a, n in enumerate(RET_ROPE_PAIRS):
        freqs = ROPE_BASE ** (-jnp.arange(n, dtype=F32) / n)
        angs.append(pos[:, a:a + 1] * freqs)
    ang = jnp.concatenate(angs, axis=-1)[None, :, None, :]
    cos, sin = jnp.cos(ang), jnp.sin(ang)
    t1, t2 = jnp.split(t, 2, axis=-1)
    return jnp.concatenate([t1 * cos - t2 * sin, t1 * sin + t2 * cos], axis=-1)


def _ret_stream(p, pos, log_gamma):
    B, L, _ = p['ret_v'].shape
    q, k = jnp.split(p['ret_qk'].astype(F32), 2, axis=-1)
    q = jnp.transpose(_rope(q.reshape(B, L, RET_H, RET_DK), pos), (0, 2, 1, 3))
    k = jnp.transpose(_rope(k.reshape(B, L, RET_H, RET_DK), pos), (0, 2, 1, 3)) * RET_DK ** -0.5
    v = _heads(p['ret_v'].astype(F32), RET_H)
    g = jnp.broadcast_to(log_gamma[None, :, None], (B, RET_H, L))
    return (q, k, v, g)


def _retention_mixer(p_ctx, p_lat, norm_w):
    Lc = p_ctx['ret_v'].shape[1]
    Ll = p_lat['ret_v'].shape[1]
    log_gamma = jnp.log(1.0 - jnp.power(2.0, -5.0 - jnp.arange(RET_H, dtype=F32)))
    c_args = _ret_stream(p_ctx, _ctx_positions(Lc), log_gamma)
    l_args = _ret_stream(p_lat, _latent_positions(Ll, Lc), log_gamma)
    s0 = jnp.zeros((c_args[0].shape[0], RET_H, RET_DK, RET_DV), F32)
    oc, ol = _bidirectional(_retention_chunk_scan, c_args, c_args, l_args, l_args, s0)
    return _head_norm(oc, norm_w), _head_norm(ol, norm_w)


def _gla_stream(p, w2, b2):
    B, L, _ = p['gla_v'].shape
    q, k = jnp.split(p['gla_qk'].astype(F32), 2, axis=-1)
    q = _heads(q, GLA_H) * GLA_DK ** -0.5
    k = _heads(k, GLA_H)
    v = _heads(p['gla_v'].astype(F32), GLA_H)
    lr = p['gla_lr'].astype(F32).reshape(B, L, 2, GLA_RANK)
    gk = jax.nn.log_sigmoid(jnp.einsum('blsr,srk->sblk', lr, w2.astype(F32))
                            + b2.astype(F32)[:, None, None, :]) / GLA_TAU
    gk = jnp.transpose(gk.reshape(2, B, L, GLA_H, GLA_DK), (0, 1, 3, 2, 4))
    return (q, k, v, gk[0]), (q, k, v, gk[1])


def _gla_mixer(p_ctx, p_lat, w2, b2, norm_w):
    cf, cb = _gla_stream(p_ctx, w2, b2)
    lf, lb = _gla_stream(p_lat, w2, b2)
    s0 = jnp.zeros((cf[0].shape[0], GLA_H, GLA_DK, GLA_DV), F32)
    oc, ol = _bidirectional(_gla_chunk_scan, cf, cb, lf, lb, s0)
    return _head_norm(oc, norm_w), _head_norm(ol, norm_w)


def _merge_branches(h, ys, gates, w_branch, w_merge, b_merge, w_out):
    out = None
    for i in range(N_BRANCH):
        y = ys[i].astype(h.dtype) * jax.nn.silu(gates[i])
        term = jax.nn.sigmoid(h @ w_merge[i] + b_merge[i]) * (y @ w_branch[i])
        out = term if out is None else out + term
    return out @ w_out


def _layer(x, ctx, c, c_ctx, with_ctx_out, norm_w, ada_w, ada_b, w_in, hy_conv_w, hy_conv_b,
           hy_w1, hy_b1, hy_w2, hy_b2, hy_w3, hy_freq, hy_skip, gdn_conv_w, gdn_a_log,
           gdn_dt_bias, gdn_norm, ret_norm, gla_w2, gla_b2, gla_norm, w_branch, w_merge,
           b_merge, w_out):
    shift_l, scale_l, gate_l = jnp.split((jax.nn.silu(c) @ ada_w + ada_b)[:, None, :], 3, axis=-1)
    shift_c, scale_c, gate_c = jnp.split(jax.nn.silu(c_ctx) @ ada_w + ada_b, 3, axis=-1)
    h_lat = _rmsnorm(x, norm_w) * (1.0 + scale_l) + shift_l
    h_ctx = _rmsnorm(ctx, norm_w) * (1.0 + scale_c) + shift_c
    p_lat = _split_in(h_lat @ w_in)
    p_ctx = _split_in(h_ctx @ w_in)

    gdn_c, gdn_l = _gdn_mixer(p_ctx, p_lat, gdn_conv_w, gdn_a_log, gdn_dt_bias, gdn_norm)
    ret_c, ret_l = _retention_mixer(p_ctx, p_lat, ret_norm)
    gla_c, gla_l = _gla_mixer(p_ctx, p_lat, gla_w2, gla_b2, gla_norm)
    hyena = lambda z: _hyena_mixer(z, hy_conv_w, hy_conv_b, hy_w1, hy_b1, hy_w2, hy_b2,
                                   hy_w3, hy_freq, hy_skip)
    merge = lambda h, p, ys: _merge_branches(
        h, ys, (p['hy_gate'], p['gdn_gate'], p['ret_gate'], p['gla_gate']),
        w_branch, w_merge, b_merge, w_out)

    x = x + gate_l * merge(h_lat, p_lat, (hyena(p_lat['hy_proj']), gdn_l, ret_l, gla_l))
    if with_ctx_out:
        ctx = ctx + gate_c * merge(h_ctx, p_ctx, (hyena(p_ctx['hy_proj']), gdn_c, ret_c, gla_c))
    return x, ctx


def setup_inputs(seed: int = 0) -> dict:
    key = jax.random.key(seed)
    ks = iter(jax.random.split(key, 32))

    def nrm(shape, scale):
        return jax.random.normal(next(ks), shape, F32) * scale

    D = D_MODEL
    x = nrm((BATCH, SEQ, D), 1.0)
    c = nrm((BATCH, D), 1.0)
    ctx = nrm((BATCH, CTX_LEN, D), 1.0)
    c_ctx = nrm((D,), 1.0)
    norm_w = 1.0 + nrm((DEPTH, D), 0.05)
    ada_w = nrm((DEPTH, D, 3 * D), 0.5 * D ** -0.5)
    ada_b = nrm((DEPTH, 3 * D), 0.02)
    w_in = nrm((DEPTH, D, N_IN), D ** -0.5)
    hy_conv_w = nrm((DEPTH, HY_SHORT, 3 * HY_W), HY_SHORT ** -0.5)
    hy_conv_b = nrm((DEPTH, 3 * HY_W), 0.02)
    hy_w1 = nrm((DEPTH, HY_EMB, HY_HID), HY_EMB ** -0.5)
    hy_b1 = nrm((DEPTH, HY_HID), 0.1)
    hy_w2 = nrm((DEPTH, HY_INNER, HY_HID, HY_HID), HY_HID ** -0.5)
    hy_b2 = nrm((DEPTH, HY_INNER, HY_HID), 0.1)
    hy_w3 = nrm((DEPTH, HY_HID, HY_ORDER * 2 * HY_W), HY_HID ** -0.5)
    hy_freq = 1.0 + nrm((DEPTH, HY_INNER + 1, HY_HID), 0.1)
    hy_skip = nrm((DEPTH, HY_ORDER, HY_W), 0.5)
    gdn_conv_w = nrm((DEPTH, GDN_SHORT, 3 * GDN_W), GDN_SHORT ** -0.5)
    gdn_a_log = jnp.log(jax.random.uniform(next(ks), (DEPTH, 2, GDN_H), F32, 1.0, 16.0))
    dt = jnp.exp(jax.random.uniform(next(ks), (DEPTH, 2, GDN_H), F32,
                                    math.log(1e-3), math.log(1e-1)))
    gdn_dt_bias = dt + jnp.log(-jnp.expm1(-dt))
    gdn_norm = 1.0 + nrm((DEPTH, GDN_DV), 0.05)
    ret_norm = 1.0 + nrm((DEPTH, RET_DV), 0.05)
    gla_w2 = nrm((DEPTH, 2, GLA_RANK, GLA_H * GLA_DK), GLA_RANK ** -0.5)
    gla_b2 = nrm((DEPTH, 2, GLA_H * GLA_DK), 0.5)
    gla_norm = 1.0 + nrm((DEPTH, GLA_DV), 0.05)
    w_branch = nrm((DEPTH, N_BRANCH, BR_W, D), BR_W ** -0.5)
    w_merge = nrm((DEPTH, N_BRANCH, D, D), D ** -0.5)
    b_merge = nrm((DEPTH, N_BRANCH, D), 0.02)
    w_out = nrm((DEPTH, D, D), D ** -0.5)
    final_norm = 1.0 + nrm((D,), 0.05)
    return {'x': x, 'c': c, 'ctx': ctx, 'c_ctx': c_ctx, 'norm_w': norm_w, 'ada_w': ada_w,
            'ada_b': ada_b, 'w_in': w_in, 'hy_conv_w': hy_conv_w, 'hy_conv_b': hy_conv_b,
            'hy_w1': hy_w1, 'hy_b1': hy_b1, 'hy_w2': hy_w2, 'hy_b2': hy_b2, 'hy_w3': hy_w3,
            'hy_freq': hy_freq, 'hy_skip': hy_skip, 'gdn_conv_w': gdn_conv_w,
            'gdn_a_log': gdn_a_log, 'gdn_dt_bias': gdn_dt_bias, 'gdn_norm': gdn_norm,
            'ret_norm': ret_norm, 'gla_w2': gla_w2, 'gla_b2': gla_b2, 'gla_norm': gla_norm,
            'w_branch': w_branch, 'w_merge': w_merge, 'b_merge': b_merge, 'w_out': w_out,
            'final_norm': final_norm}


def reference(x, c, ctx, c_ctx, norm_w, ada_w, ada_b, w_in, hy_conv_w, hy_conv_b, hy_w1, hy_b1,
              hy_w2, hy_b2, hy_w3, hy_freq, hy_skip, gdn_conv_w, gdn_a_log, gdn_dt_bias,
              gdn_norm, ret_norm, gla_w2, gla_b2, gla_norm, w_branch, w_merge, b_merge, w_out,
              final_norm):
    layer_params = (norm_w, ada_w, ada_b, w_in, hy_conv_w, hy_conv_b, hy_w1, hy_b1, hy_w2,
                    hy_b2, hy_w3, hy_freq, hy_skip, gdn_conv_w, gdn_a_log, gdn_dt_bias,
                    gdn_norm, ret_norm, gla_w2, gla_b2, gla_norm, w_branch, w_merge, b_merge,
                    w_out)
    for l in range(DEPTH):
        x, ctx = _layer(x, ctx, c, c_ctx, l < DEPTH - 1, *[p[l] for p in layer_params])
    return _rmsnorm(x, final_norm)
```

```python
import functools
import math

import jax
import jax.numpy as jnp
from jax import lax
from jax.experimental import pallas as pl
from jax.experimental.pallas import tpu as pltpu

D_MODEL = 2048
DEPTH = 4
CTX_LEN = 256
GRID_W = 64
F32 = jnp.float32
BF16 = jnp.bfloat16
EPS = 1e-6
CHUNK = 64

N_BRANCH = 4
BR_W = D_MODEL // N_BRANCH

HY_W = BR_W
HY_ORDER = 2
HY_EMB = 33
HY_BANDS = (HY_EMB - 1) // 2
HY_INNER = 2
HY_FAST_DECAY = 0.3
HY_SLOW_DECAY = 1.5
HY_TARGET = 1e-2
HY_MIN_DECAY = math.log(HY_TARGET) / HY_SLOW_DECAY
HY_MAX_DECAY = math.log(HY_TARGET) / HY_FAST_DECAY

GDN_H = 4
GDN_DK = BR_W // GDN_H
GDN_DV = BR_W // GDN_H
GDN_W = GDN_H * GDN_DV

RET_H = 4
RET_DK = BR_W // (2 * RET_H)
RET_DV = BR_W // RET_H
RET_W = RET_H * RET_DV
RET_ROPE_PAIRS = (8, 12, 12)
ROPE_BASE = 10000.0

GLA_H = 4
GLA_DK = BR_W // (2 * GLA_H)
GLA_DV = BR_W // GLA_H
GLA_W = GLA_H * GLA_DV
GLA_RANK = 16
GLA_TAU = 16.0

IN_SPLITS = (
    ('hy_proj', 3 * HY_W), ('hy_gate', HY_W),
    ('gdn_qkv', 3 * GDN_W), ('gdn_a', 2 * GDN_H), ('gdn_b', 2 * GDN_H), ('gdn_gate', GDN_W),
    ('ret_qk', 2 * RET_H * RET_DK), ('ret_v', RET_W), ('ret_gate', RET_W),
    ('gla_qk', 2 * GLA_H * GLA_DK), ('gla_v', GLA_W), ('gla_lr', 2 * GLA_RANK), ('gla_gate', GLA_W),
)
N_IN = sum(size for _, size in IN_SPLITS)

V7X_LANES = 128
VMEM_LIMIT_BYTES = 48 * 1024 * 1024


def _mm_kernel(a_ref, b_ref, o_ref):
    o_ref[...] = jnp.dot(a_ref[...], b_ref[...], preferred_element_type=F32).astype(o_ref.dtype)


def _pick_tile(n, cands):
    for c in cands:
        if n % c == 0:
            return c
    return n


def _matmul(a, b, out_dtype=F32):
    M, K = a.shape
    _, N = b.shape
    tm = _pick_tile(M, (1024, 512, 256, 128, 64, 32, 16, 8))
    tn = _pick_tile(N, (512, 384, 256, 128))
    return pl.pallas_call(
        _mm_kernel,
        grid=(M // tm, N // tn),
        in_specs=[pl.BlockSpec((tm, K), lambda i, j: (i, 0)),
                  pl.BlockSpec((K, tn), lambda i, j: (0, j))],
        out_specs=pl.BlockSpec((tm, tn), lambda i, j: (i, j)),
        out_shape=jax.ShapeDtypeStruct((M, N), out_dtype),
        compiler_params=pltpu.CompilerParams(
            dimension_semantics=("parallel", "arbitrary"),
            vmem_limit_bytes=VMEM_LIMIT_BYTES),
        name="matmul",
    )(a, b)


def _mm(a, b):
    lead = a.shape[:-1]
    out = _matmul(a.reshape(-1, a.shape[-1]).astype(BF16), b.astype(BF16))
    return out.reshape(*lead, b.shape[-1])


def _rmsnorm(x, w):
    return x * lax.rsqrt(jnp.mean(x * x, axis=-1, keepdims=True) + EPS) * w


def _head_norm(o, w):
    o = o * lax.rsqrt(jnp.mean(o * o, axis=-1, keepdims=True) + EPS) * w
    B, H, L, d = o.shape
    return jnp.transpose(o, (0, 2, 1, 3)).reshape(B, L, H * d)


def _l2norm(t):
    return t * lax.rsqrt(jnp.sum(t * t, axis=-1, keepdims=True) + EPS)


def _heads(t, H):
    B, L, W = t.shape
    return jnp.transpose(t.reshape(B, L, H, W // H), (0, 2, 1, 3))


def _split_in(u):
    parts = {}
    off = 0
    for name, size in IN_SPLITS:
        parts[name] = u[..., off:off + size]
        off += size
    return parts


def _short_conv(x, w, b=None):
    K, C = w.shape
    p = K // 2
    y = lax.conv_general_dilated(x, w[:, None, :].astype(x.dtype), (1,), [(p, p)],
                                 dimension_numbers=('NWC', 'WIO', 'NWC'), feature_group_count=C)
    return y if b is None else y + b


def _causal_mask(strict=False):
    return jnp.tril(jnp.ones((CHUNK, CHUNK), bool), -1 if strict else 0)


def _to_chunks(t):
    return t.reshape(t.shape[0], t.shape[1], t.shape[2] // CHUNK, CHUNK, *t.shape[3:])


def _chunks_first(*ts):
    return tuple(jnp.moveaxis(t, 2, 0) for t in ts)


def _scalar_decay(G):
    return jnp.exp(jnp.where(_causal_mask(), G[..., :, None] - G[..., None, :], -jnp.inf))


def _retention_chunk_scan(q, k, v, g, s0):
    qc, kc, vc = _to_chunks(q), _to_chunks(k), _to_chunks(v)
    G = jnp.cumsum(_to_chunks(g), axis=-1)
    scores = jnp.einsum('bhnid,bhnjd->bhnij', qc, kc) * _scalar_decay(G)
    intra = jnp.einsum('bhnij,bhnjv->bhniv', scores, vc)
    q_in = qc * jnp.exp(G)[..., None]
    k_out = kc * jnp.exp(G[..., -1:] - G)[..., None]
    a_end = jnp.exp(G[..., -1])

    def step(s, xs):
        qi, ki, vi, ai = xs
        o = jnp.einsum('bhid,bhdv->bhiv', qi, s)
        s = s * ai[..., None, None] + jnp.einsum('bhid,bhiv->bhdv', ki, vi)
        return s, o

    s, inter = lax.scan(step, s0, _chunks_first(q_in, k_out, vc, a_end))
    o = intra + jnp.moveaxis(inter, 0, 2)
    return o.reshape(o.shape[0], o.shape[1], -1, o.shape[-1]), s


def _gated_delta_chunk_scan(q, k, v, beta, g, s0):
    qc, kc, vc, bc = _to_chunks(q), _to_chunks(k), _to_chunks(v), _to_chunks(beta)
    G = jnp.cumsum(_to_chunks(g), axis=-1)
    decay = _scalar_decay(G)
    kb = kc * bc[..., None]
    lmat = jnp.where(_causal_mask(strict=True),
                     jnp.einsum('bhnid,bhnjd->bhnij', kb, kc) * decay, 0.0)
    rhs = jnp.concatenate([vc * bc[..., None], kb * jnp.exp(G)[..., None]], axis=-1)
    sol = lax.linalg.triangular_solve(lmat, rhs, left_side=True, lower=True, unit_diagonal=True)
    dv = vc.shape[-1]
    u, w = sol[..., :dv], sol[..., dv:]
    scores = jnp.einsum('bhnid,bhnjd->bhnij', qc, kc) * decay
    q_in = qc * jnp.exp(G)[..., None]
    k_out = kc * jnp.exp(G[..., -1:] - G)[..., None]
    a_end = jnp.exp(G[..., -1])

    def step(s, xs):
        ui, wi, si, qi, ki, ai = xs
        v_new = ui - jnp.einsum('bhid,bhdv->bhiv', wi, s)
        o = jnp.einsum('bhid,bhdv->bhiv', qi, s) + jnp.einsum('bhij,bhjv->bhiv', si, v_new)
        s = s * ai[..., None, None] + jnp.einsum('bhid,bhiv->bhdv', ki, v_new)
        return s, o

    s, o = lax.scan(step, s0, _chunks_first(u, w, scores, q_in, k_out, a_end))
    o = jnp.moveaxis(o, 0, 2)
    return o.reshape(o.shape[0], o.shape[1], -1, o.shape[-1]), s


def _gla_chunk_scan(q, k, v, gk, s0):
    qc, kc, vc = _to_chunks(q), _to_chunks(k), _to_chunks(v)
    G = jnp.cumsum(_to_chunks(gk), axis=3)
    mask = _causal_mask()[:, :, None]

    def step(s, xs):
        qi, ki, vi, Gi = xs
        rel = jnp.exp(jnp.where(mask, Gi[:, :, :, None, :] - Gi[:, :, None, :, :], -jnp.inf))
        scores = jnp.einsum('bhid,bhjd,bhijd->bhij', qi, ki, rel)
        g_end = Gi[:, :, -1:, :]
        o = (jnp.einsum('bhid,bhdv->bhiv', qi * jnp.exp(Gi), s)
             + jnp.einsum('bhij,bhjv->bhiv', scores, vi))
        s = (s * jnp.exp(g_end)[:, :, 0, :, None]
             + jnp.einsum('bhid,bhiv->bhdv', ki * jnp.exp(g_end - Gi), vi))
        return s, o

    s, o = lax.scan(step, s0, _chunks_first(qc, kc, vc, G))
    o = jnp.moveaxis(o, 0, 2)
    return o.reshape(o.shape[0], o.shape[1], -1, o.shape[-1]), s


def _bidirectional(scan_fn, ctx_f, ctx_b, lat_f, lat_b, s0):
    rev = lambda ts: tuple(jnp.flip(t, axis=2) for t in ts)
    oc_f, sc_f = scan_fn(*ctx_f, s0)
    oc_b, sc_b = scan_fn(*rev(ctx_b), s0)
    ol_f, _ = scan_fn(*lat_f, sc_f)
    ol_b, _ = scan_fn(*rev(lat_b), sc_b)
    return oc_f + jnp.flip(oc_b, axis=2), ol_f + jnp.flip(ol_b, axis=2)


def _hyena_filters(L, w1, b1, w2, b2, w3, freq):
    t = jnp.linspace(0.0, 1.0, L, dtype=F32)[:, None]
    w = 2.0 * math.pi * jnp.arange(L, dtype=F32) / L
    f = jnp.linspace(1e-4, HY_BANDS - 1, HY_BANDS, dtype=F32)
    ang = w[:, None] * f[None, :]
    z = jnp.concatenate([t, jnp.cos(ang), -jnp.sin(ang)], axis=-1)
    hdn = jnp.sin(freq[0] * (z @ w1 + b1))
    for i in range(HY_INNER):
        hdn = jnp.sin(freq[i + 1] * (hdn @ w2[i] + b2[i]))
    h = (hdn @ w3).reshape(L, HY_ORDER, 2, HY_W)
    deltas = jnp.abs(jnp.linspace(HY_MIN_DECAY, HY_MAX_DECAY, HY_W, dtype=F32))
    h = h * jnp.exp(-t[:, :, None, None] * deltas)
    hf, hb = h[:, :, 0], h[:, :, 1]
    k2 = jnp.concatenate([hf, jnp.zeros_like(hf[:1]), jnp.flip(hb[1:], axis=0)], axis=0)
    k2 = k2 * lax.rsqrt(jnp.sum(k2 * k2, axis=0, keepdims=True) + EPS)
    return jnp.fft.rfft(k2, axis=0)


def _fft_long_conv(z, kf):
    L = z.shape[1]
    zf = jnp.fft.rfft(z, n=2 * L, axis=1)
    return jnp.fft.irfft(zf * kf[None], n=2 * L, axis=1)[:, :L]


def _hyena_mixer(z, conv_w, conv_b, w1, b1, w2, b2, w3, freq, skip):
    L = z.shape[1]
    z = _short_conv(z, conv_w, conv_b)
    v, x1, x2 = jnp.split(z, 3, axis=-1)
    kf = _hyena_filters(L, w1, b1, w2, b2, w3, freq)
    y = x1 * (_fft_long_conv(v, kf[:, 0]) + skip[0] * v)
    y = x2 * (_fft_long_conv(y, kf[:, 1]) + skip[1] * y)
    return y


def _gdn_stream(p, conv_w, a_log, dt_bias):
    B, L, _ = p['gdn_qkv'].shape
    qkv = jax.nn.silu(_short_conv(p['gdn_qkv'], conv_w))
    q, k, v = jnp.split(qkv, 3, axis=-1)
    q = _l2norm(_heads(q, GDN_H)) * GDN_DK ** -0.5
    k = _l2norm(_heads(k, GDN_H))
    v = _heads(v, GDN_H)
    a = p['gdn_a'].reshape(B, L, 2, GDN_H)
    b = p['gdn_b'].reshape(B, L, 2, GDN_H)
    g = -jnp.exp(a_log) * jax.nn.softplus(a + dt_bias)
    g = jnp.transpose(g, (2, 0, 3, 1))
    beta = jnp.transpose(jax.nn.sigmoid(b), (2, 0, 3, 1))
    return (q, k, v, beta[0], g[0]), (q, k, v, beta[1], g[1])


def _gdn_mixer(p_ctx, p_lat, conv_w, a_log, dt_bias, norm_w):
    cf, cb = _gdn_stream(p_ctx, conv_w, a_log, dt_bias)
    lf, lb = _gdn_stream(p_lat, conv_w, a_log, dt_bias)
    s0 = jnp.zeros((cf[0].shape[0], GDN_H, GDN_DK, GDN_DV), F32)
    oc, ol = _bidirectional(_gated_delta_chunk_scan, cf, cb, lf, lb, s0)
    return _head_norm(oc, norm_w), _head_norm(ol, norm_w)


def _ctx_positions(L):
    t = jnp.arange(L, dtype=F32)
    zero = jnp.zeros((L,), F32)
    return jnp.stack([t, zero, zero], axis=-1)


def _latent_positions(L, offset):
    rows = L // GRID_W
    r = jnp.repeat(jnp.arange(rows, dtype=F32), GRID_W)
    col = jnp.broadcast_to(jnp.arange(GRID_W, dtype=F32), (rows, GRID_W)).reshape(-1)
    return jnp.stack([jnp.full((L,), offset, F32), r, col], axis=-1)


def _rope(t, pos):
    angs = []
    for a, n in enumerate(RET_ROPE_PAIRS):
        freqs = ROPE_BASE ** (-jnp.arange(n, dtype=F32) / n)
        angs.append(pos[:, a:a + 1] * freqs)
    ang = jnp.concatenate(angs, axis=-1)[None, :, None, :]
    cos, sin = jnp.cos(ang), jnp.sin(ang)
    t1, t2 = jnp.split(t, 2, axis=-1)
    return jnp.concatenate([t1 * cos - t2 * sin, t1 * sin + t2 * cos], axis=-1)


def _ret_stream(p, pos, log_gamma):
    B, L, _ = p['ret_v'].shape
    q, k = jnp.split(p['ret_qk'], 2, axis=-1)
    q = jnp.transpose(_rope(q.reshape(B, L, RET_H, RET_DK), pos), (0, 2, 1, 3))
    k = jnp.transpose(_rope(k.reshape(B, L, RET_H, RET_DK), pos), (0, 2, 1, 3)) * RET_DK ** -0.5
    v = _heads(p['ret_v'], RET_H)
    g = jnp.broadcast_to(log_gamma[None, :, None], (B, RET_H, L))
    return (q, k, v, g)


def _retention_mixer(p_ctx, p_lat, norm_w):
    Lc = p_ctx['ret_v'].shape[1]
    Ll = p_lat['ret_v'].shape[1]
    log_gamma = jnp.log(1.0 - jnp.power(2.0, -5.0 - jnp.arange(RET_H, dtype=F32)))
    c_args = _ret_stream(p_ctx, _ctx_positions(Lc), log_gamma)
    l_args = _ret_stream(p_lat, _latent_positions(Ll, Lc), log_gamma)
    s0 = jnp.zeros((c_args[0].shape[0], RET_H, RET_DK, RET_DV), F32)
    oc, ol = _bidirectional(_retention_chunk_scan, c_args, c_args, l_args, l_args, s0)
    return _head_norm(oc, norm_w), _head_norm(ol, norm_w)


def _gla_stream(p, w2, b2):
    B, L, _ = p['gla_v'].shape
    q, k = jnp.split(p['gla_qk'], 2, axis=-1)
    q = _heads(q, GLA_H) * GLA_DK ** -0.5
    k = _heads(k, GLA_H)
    v = _heads(p['gla_v'], GLA_H)
    lr = p['gla_lr'].reshape(B, L, 2, GLA_RANK)
    gk = jax.nn.log_sigmoid(jnp.einsum('blsr,srk->sblk', lr, w2)
                            + b2[:, None, None, :]) / GLA_TAU
    gk = jnp.transpose(gk.reshape(2, B, L, GLA_H, GLA_DK), (0, 1, 3, 2, 4))
    return (q, k, v, gk[0]), (q, k, v, gk[1])


def _gla_mixer(p_ctx, p_lat, w2, b2, norm_w):
    cf, cb = _gla_stream(p_ctx, w2, b2)
    lf, lb = _gla_stream(p_lat, w2, b2)
    s0 = jnp.zeros((cf[0].shape[0], GLA_H, GLA_DK, GLA_DV), F32)
    oc, ol = _bidirectional(_gla_chunk_scan, cf, cb, lf, lb, s0)
    return _head_norm(oc, norm_w), _head_norm(ol, norm_w)


def _merge_branches(h, ys, gates, w_branch, w_merge, b_merge, w_out):
    out = None
    for i in range(N_BRANCH):
        y = ys[i] * jax.nn.silu(gates[i])
        term = jax.nn.sigmoid(_mm(h, w_merge[i]) + b_merge[i]) * _mm(y, w_branch[i])
        out = term if out is None else out + term
    return _mm(out, w_out)


def _layer(x, ctx, c, c_ctx, with_ctx_out, norm_w, ada_w, ada_b, w_in, hy_conv_w, hy_conv_b,
           hy_w1, hy_b1, hy_w2, hy_b2, hy_w3, hy_freq, hy_skip, gdn_conv_w, gdn_a_log,
           gdn_dt_bias, gdn_norm, ret_norm, gla_w2, gla_b2, gla_norm, w_branch, w_merge,
           b_merge, w_out):
    shift_l, scale_l, gate_l = jnp.split((jax.nn.silu(c) @ ada_w + ada_b)[:, None, :], 3, axis=-1)
    shift_c, scale_c, gate_c = jnp.split(jax.nn.silu(c_ctx) @ ada_w + ada_b, 3, axis=-1)
    h_lat = _rmsnorm(x, norm_w) * (1.0 + scale_l) + shift_l
    h_ctx = _rmsnorm(ctx, norm_w) * (1.0 + scale_c) + shift_c
    w_in = jnp.pad(w_in, ((0, 0), (0, -N_IN % V7X_LANES)))
    p_lat = _split_in(_mm(h_lat, w_in))
    p_ctx = _split_in(_mm(h_ctx, w_in))

    gdn_c, gdn_l = _gdn_mixer(p_ctx, p_lat, gdn_conv_w, gdn_a_log, gdn_dt_bias, gdn_norm)
    ret_c, ret_l = _retention_mixer(p_ctx, p_lat, ret_norm)
    gla_c, gla_l = _gla_mixer(p_ctx, p_lat, gla_w2, gla_b2, gla_norm)
    hyena = lambda z: _hyena_mixer(z, hy_conv_w, hy_conv_b, hy_w1, hy_b1, hy_w2, hy_b2,
                                   hy_w3, hy_freq, hy_skip)
    merge = lambda h, p, ys: _merge_branches(
        h, ys, (p['hy_gate'], p['gdn_gate'], p['ret_gate'], p['gla_gate']),
        w_branch, w_merge, b_merge, w_out)

    x = x + gate_l * merge(h_lat, p_lat, (hyena(p_lat['hy_proj']), gdn_l, ret_l, gla_l))
    if with_ctx_out:
        ctx = ctx + gate_c * merge(h_ctx, p_ctx, (hyena(p_ctx['hy_proj']), gdn_c, ret_c, gla_c))
    return x, ctx


def kernel(x, c, ctx, c_ctx, norm_w, ada_w, ada_b, w_in, hy_conv_w, hy_conv_b, hy_w1, hy_b1,
           hy_w2, hy_b2, hy_w3, hy_freq, hy_skip, gdn_conv_w, gdn_a_log, gdn_dt_bias,
           gdn_norm, ret_norm, gla_w2, gla_b2, gla_norm, w_branch, w_merge, b_merge, w_out,
           final_norm):
    layer_params = (norm_w, ada_w, ada_b, w_in, hy_conv_w, hy_conv_b, hy_w1, hy_b1, hy_w2,
                    hy_b2, hy_w3, hy_freq, hy_skip, gdn_conv_w, gdn_a_log, gdn_dt_bias,
                    gdn_norm, ret_norm, gla_w2, gla_b2, gla_norm, w_branch, w_merge, b_merge,
                    w_out)
    for l in range(DEPTH):
        x, ctx = _layer(x, ctx, c, c_ctx, l < DEPTH - 1, *[p[l] for p in layer_params])
    return _rmsnorm(x, final_norm)
```

```python
import functools
import math

import numpy as np
import jax
import jax.numpy as jnp
from jax import lax
from jax.experimental import pallas as pl
from jax.experimental.pallas import tpu as pltpu

D_MODEL = 2048
DEPTH = 4
CTX_LEN = 256
GRID_W = 64
F32 = jnp.float32
BF16 = jnp.bfloat16
EPS = 1e-6
CHUNK = 64

N_BRANCH = 4
BR_W = D_MODEL // N_BRANCH

HY_W = BR_W
HY_ORDER = 2
HY_EMB = 33
HY_BANDS = (HY_EMB - 1) // 2
HY_INNER = 2
HY_FAST_DECAY = 0.3
HY_SLOW_DECAY = 1.5
HY_TARGET = 1e-2
HY_MIN_DECAY = math.log(HY_TARGET) / HY_SLOW_DECAY
HY_MAX_DECAY = math.log(HY_TARGET) / HY_FAST_DECAY

N_HEAD = 4
GDN_DK = BR_W // N_HEAD
GDN_DV = BR_W // N_HEAD
RET_DK = BR_W // (2 * N_HEAD)
RET_DV = BR_W // N_HEAD
RET_ROPE_PAIRS = (8, 12, 12)
ROPE_BASE = 10000.0
GLA_DK = BR_W // (2 * N_HEAD)
GLA_DV = BR_W // N_HEAD
GLA_RANK = 16
GLA_TAU = 16.0

V7X_LANES = 128
VMEM_LIMIT_BYTES = 56 * 1024 * 1024

SCAN_BLK = CTX_LEN
CPB = SCAN_BLK // CHUNK
COLW = BR_W

CB_HY_V, CB_HY_X1, CB_HY_X2, CB_HY_GATE = 0, 1, 2, 3
CB_GDN_Q, CB_GDN_GATE = 4, 7
CB_RET_QK, CB_RET_QKROT, CB_RET_V, CB_RET_GATE = 8, 9, 10, 11
CB_GLA_QK, CB_GLA_V, CB_GLA_GATE = 12, 13, 14
N_COLBLK = 15
SM_A, SM_B, SM_LR = 0, 2 * N_HEAD, 4 * N_HEAD


def _dot(a, b):
    return jnp.dot(a, b, preferred_element_type=F32)


def _dot_nt(a, b):
    return lax.dot_general(a, b, (((1,), (1,)), ((), ())), preferred_element_type=F32)


def _dot_tn(a, b):
    return lax.dot_general(a, b, (((0,), (0,)), ((), ())), preferred_element_type=F32)


def _split_bf16(x, terms):
    parts = []
    r = x
    for t in range(terms):
        p = r.astype(BF16)
        parts.append(p)
        if t + 1 < terms:
            r = r - p.astype(F32)
    return parts


def _dot_f32(a, b):
    a1, a2 = _split_bf16(a, 2)
    b1, b2 = _split_bf16(b, 2)
    return _dot(a1, b1) + (_dot(a1, b2) + _dot(a2, b1))


def _dot_const(c, x, terms):
    out = None
    for p in _split_bf16(x, terms):
        t = _dot(c, p)
        out = t if out is None else out + t
    return out


def _dot_tn_const(x, c, terms):
    out = None
    for p in _split_bf16(x, terms):
        t = _dot_tn(p, c)
        out = t if out is None else out + t
    return out


def _silu(x):
    return x * jax.nn.sigmoid(x)


def _cparams(sem):
    return pltpu.CompilerParams(dimension_semantics=sem, vmem_limit_bytes=VMEM_LIMIT_BYTES)


def _mm_kernel(a_ref, b_ref, o_ref):
    o_ref[...] = _dot(a_ref[...], b_ref[...]).astype(o_ref.dtype)


def _pick_tile(n, cands):
    for c in cands:
        if n % c == 0:
            return c
    return n


def _matmul(a, b, out_dtype=F32):
    M, K = a.shape
    _, N = b.shape
    tm = _pick_tile(M, (1024, 512, 256, 128, 64, 32, 16, 8))
    tn = _pick_tile(N, (512, 384, 256, 128))
    return pl.pallas_call(
        _mm_kernel,
        grid=(M // tm, N // tn),
        in_specs=[pl.BlockSpec((tm, K), lambda i, j: (i, 0)),
                  pl.BlockSpec((K, tn), lambda i, j: (0, j))],
        out_specs=pl.BlockSpec((tm, tn), lambda i, j: (i, j)),
        out_shape=jax.ShapeDtypeStruct((M, N), out_dtype),
        compiler_params=_cparams(("parallel", "arbitrary")),
        name="matmul",
    )(a, b)


def _mod_row(n_ctx_blk, blk_per_seq):
    return lambda i: jnp.where(i < n_ctx_blk, 0, 1 + (i - n_ctx_blk) // blk_per_seq)


def _inproj_kernel(x_ref, nw_ref, sc_ref, sh_ref, w_ref, p_ref, h_ref, hs_ref):
    @pl.when(pl.program_id(1) == 0)
    def _():
        x = x_ref[...]
        y = x * lax.rsqrt(jnp.mean(x * x, axis=-1, keepdims=True) + EPS) * nw_ref[...]
        hs_ref[...] = (y * (1.0 + sc_ref[0]) + sh_ref[0]).astype(BF16)
        h_ref[...] = hs_ref[...]

    p_ref[...] = _dot(hs_ref[...], w_ref[...])


def _inproj(x, norm_w, scale, shift, w, n_ctx_rows, seq_len):
    T, D = x.shape
    N = w.shape[1]
    tm, tn = 512, COLW
    row = _mod_row(n_ctx_rows // tm, seq_len // tm)
    return pl.pallas_call(
        _inproj_kernel,
        grid=(T // tm, N // tn),
        in_specs=[pl.BlockSpec((tm, D), lambda i, j: (i, 0)),
                  pl.BlockSpec((1, D), lambda i, j: (0, 0)),
                  pl.BlockSpec((1, 1, D), lambda i, j: (row(i), 0, 0)),
                  pl.BlockSpec((1, 1, D), lambda i, j: (row(i), 0, 0)),
                  pl.BlockSpec((D, tn), lambda i, j: (0, j))],
        out_specs=[pl.BlockSpec((tm, tn), lambda i, j: (i, j)),
                   pl.BlockSpec((tm, D), lambda i, j: (i, 0))],
        out_shape=[jax.ShapeDtypeStruct((T, N), F32), jax.ShapeDtypeStruct((T, D), BF16)],
        scratch_shapes=[pltpu.VMEM((tm, D), BF16)],
        compiler_params=_cparams(("parallel", "arbitrary")),
        name="inproj",
    )(x, norm_w.reshape(1, D), scale, shift, w)


def _merge_kernel(h_ref, y0_ref, y1_ref, y2_ref, y3_ref, wm_ref, bm_ref, wb_ref, o_ref):
    h = h_ref[...]
    acc = None
    for i, y_ref in enumerate((y0_ref, y1_ref, y2_ref, y3_ref)):
        g = jax.nn.sigmoid(_dot(h, wm_ref[i]) + bm_ref[i])
        t = g * _dot(y_ref[...], wb_ref[i])
        acc = t if acc is None else acc + t
    o_ref[...] = acc.astype(o_ref.dtype)


def _merge(h, ys, w_merge, b_merge, w_branch):
    T, D = h.shape
    tm, tn = 1024, 256
    yspec = pl.BlockSpec((tm, BR_W), lambda i, j: (i, 0))
    return pl.pallas_call(
        _merge_kernel,
        grid=(T // tm, D // tn),
        in_specs=[pl.BlockSpec((tm, D), lambda i, j: (i, 0)), yspec, yspec, yspec, yspec,
                  pl.BlockSpec((N_BRANCH, D, tn), lambda i, j: (0, 0, j)),
                  pl.BlockSpec((N_BRANCH, 1, tn), lambda i, j: (0, 0, j)),
                  pl.BlockSpec((N_BRANCH, BR_W, tn), lambda i, j: (0, 0, j))],
        out_specs=pl.BlockSpec((tm, tn), lambda i, j: (i, j)),
        out_shape=jax.ShapeDtypeStruct((T, D), BF16),
        compiler_params=_cparams(("parallel", "arbitrary")),
        name="merge",
    )(h, *ys, w_merge, b_merge.reshape(N_BRANCH, 1, D), w_branch)


def _outproj_kernel(a_ref, w_ref, x_ref, g_ref, o_ref):
    o_ref[...] = x_ref[...] + g_ref[0] * _dot(a_ref[...], w_ref[...])


def _outproj(a, w, x, gate, n_ctx_rows, seq_len):
    T, D = x.shape
    tm, tn = 1024, 512
    row = _mod_row(n_ctx_rows // tm, seq_len // tm)
    return pl.pallas_call(
        _outproj_kernel,
        grid=(T // tm, D // tn),
        in_specs=[pl.BlockSpec((tm, D), lambda i, j: (i, 0)),
                  pl.BlockSpec((D, tn), lambda i, j: (0, j)),
                  pl.BlockSpec((tm, tn), lambda i, j: (i, j)),
                  pl.BlockSpec((1, 1, tn), lambda i, j: (row(i), 0, j))],
        out_specs=pl.BlockSpec((tm, tn), lambda i, j: (i, j)),
        out_shape=jax.ShapeDtypeStruct((T, D), F32),
        compiler_params=_cparams(("parallel", "arbitrary")),
        name="outproj",
    )(a, w, x, gate)


def _final_norm_kernel(x_ref, w_ref, o_ref):
    x = x_ref[...]
    o_ref[...] = x * lax.rsqrt(jnp.mean(x * x, axis=-1, keepdims=True) + EPS) * w_ref[...]


def _final_norm(x, w, n_ctx_rows):
    T, D = x.shape
    tm = 512
    off = n_ctx_rows // tm
    return pl.pallas_call(
        _final_norm_kernel,
        grid=((T - n_ctx_rows) // tm,),
        in_specs=[pl.BlockSpec((tm, D), lambda i: (i + off, 0)),
                  pl.BlockSpec((1, D), lambda i: (0, 0))],
        out_specs=pl.BlockSpec((tm, D), lambda i: (i, 0)),
        out_shape=jax.ShapeDtypeStruct((T - n_ctx_rows, D), F32),
        compiler_params=_cparams(("parallel",)),
        name="final_norm",
    )(x, w.reshape(1, D))


def _scan_rows(n_seq, n_lat_blk, rev):
    def idx(b, j):
        jj = (n_lat_blk - j) if rev else (j - 1)
        return jnp.where(j == 0, b, n_seq + b * n_lat_blk + jj)
    return idx


def _chunk_order(rev):
    return range(CPB - 1, -1, -1) if rev else range(CPB)


def _before(rev):
    i = np.arange(CHUNK)[:, None]
    j = np.arange(CHUNK)[None, :]
    return (j > i) if rev else (j < i)


def _finalize(o, oprev_ref, gate_ref, nw_ref, r0, c0, width):
    o = o + oprev_ref[r0:r0 + CHUNK, c0:c0 + width]
    o = o * lax.rsqrt(jnp.mean(o * o, axis=-1, keepdims=True) + EPS) * nw_ref[...]
    return o * _silu(gate_ref[r0:r0 + CHUNK, c0:c0 + width])


def _scan_call(kernel, name, n_seq, n_lat_blk, rev, finalize, row_inputs, tab_inputs, const_inputs,
               extra_specs_inputs, oprev, gate_src, norm_w, scratch):
    rows = _scan_rows(n_seq, n_lat_blk, rev)
    T = row_inputs[0][0].shape[0]
    args, specs = [], []
    for arr, cb in row_inputs:
        args.append(arr)
        specs.append(pl.BlockSpec((SCAN_BLK, COLW), functools.partial(
            lambda b, j, cb: (rows(b, j), cb), cb=cb)))
    for arr in tab_inputs:
        args.append(arr)
        specs.append(pl.BlockSpec((SCAN_BLK, arr.shape[1]), lambda b, j: (
            jnp.where(j == 0, 0, 1 + ((n_lat_blk - j) if rev else (j - 1))), 0)))
    for arr, spec in extra_specs_inputs:
        args.append(arr)
        specs.append(spec)
    for arr in const_inputs:
        args.append(arr)
        specs.append(pl.BlockSpec(arr.shape, functools.partial(lambda b, j, n: (0,) * n, n=arr.ndim)))
    if finalize:
        args += [oprev, gate_src[0], norm_w.reshape(1, -1)]
        specs += [pl.BlockSpec((SCAN_BLK, COLW), lambda b, j: (rows(b, j), 0)),
                  pl.BlockSpec((SCAN_BLK, COLW), functools.partial(
                      lambda b, j, cb: (rows(b, j), cb), cb=gate_src[1])),
                  pl.BlockSpec((1, norm_w.shape[0]), lambda b, j: (0, 0))]
    return pl.pallas_call(
        kernel,
        grid=(n_seq, 1 + n_lat_blk),
        in_specs=specs,
        out_specs=pl.BlockSpec((SCAN_BLK, COLW), lambda b, j: (rows(b, j), 0)),
        out_shape=jax.ShapeDtypeStruct((T, COLW), BF16 if finalize else F32),
        scratch_shapes=scratch,
        compiler_params=_cparams(("parallel", "arbitrary")),
        name=name,
    )(*args)


def _ret_consts(rev):
    lg = np.log(1.0 - np.power(2.0, -5.0 - np.arange(N_HEAD, dtype=np.float64)))
    i = np.arange(CHUNK, dtype=np.float64)
    steps = (CHUNK - i) if rev else (i + 1.0)
    G = steps[None, :] * lg[:, None]
    mask = _before(rev) | np.eye(CHUNK, dtype=bool)
    dm = np.where(mask[None], np.exp(G[:, :, None] - G[:, None, :]), 0.0)
    qs = np.repeat(np.exp(G).T, RET_DK, axis=1)
    ks = np.repeat(np.exp(CHUNK * lg[:, None] - G).T, RET_DK, axis=1)
    a_end = tuple(float(v) for v in np.exp(CHUNK * lg))
    return (jnp.asarray(dm, F32), jnp.asarray(qs, F32), jnp.asarray(ks, F32)), a_end


def _ret_kernel(qk_ref, qkr_ref, v_ref, cos_ref, sin_ref, dm_ref, qs_ref, ks_ref, *rest,
                rev, finalize, a_end):
    if finalize:
        oprev_ref, gate_ref, nw_ref, o_ref, s_ref = rest
    else:
        o_ref, s_ref = rest

    @pl.when(pl.program_id(1) == 0)
    def _():
        s_ref[...] = jnp.zeros_like(s_ref)

    hw = N_HEAD * RET_DK
    qk = qk_ref[...] * cos_ref[...] + qkr_ref[...] * sin_ref[...]
    order = list(_chunk_order(rev))
    units = [(ci, h) for ci in range(CPB) for h in range(N_HEAD)]
    rows = lambda c: slice(c * CHUNK, (c + 1) * CHUNK)
    kcols = lambda h: slice(h * RET_DK, (h + 1) * RET_DK)
    vcols = lambda h: slice(h * RET_DV, (h + 1) * RET_DV)

    qc = [qk[rows(c), :hw] for c in order]
    kc = [qk[rows(c), hw:] for c in order]
    qb = [t.astype(BF16) for t in qc]
    kb = [t.astype(BF16) for t in kc]
    q_in = [(t * qs_ref[...]).astype(BF16) for t in qc]
    k_out = [(t * ks_ref[...]).astype(BF16) for t in kc]
    vb = [v_ref[rows(c), :].astype(BF16) for c in order]
    vh = [vb[ci][:, vcols(h)] for ci, h in units]
    sc = [(_dot_nt(qb[ci][:, kcols(h)], kb[ci][:, kcols(h)]) * dm_ref[h]).astype(BF16) for ci, h in units]
    o_intra = [_dot(s_, v_) for s_, v_ in zip(sc, vh)]
    kv = [_dot_tn(k_out[ci][:, kcols(h)], v_) for (ci, h), v_ in zip(units, vh)]

    s = [s_ref[h] for h in range(N_HEAD)]
    for ci in range(CPB):
        c = order[ci]
        for h in range(N_HEAD):
            n = ci * N_HEAD + h
            o = o_intra[n] + _dot(q_in[ci][:, kcols(h)], s[h].astype(BF16))
            s[h] = a_end[h] * s[h] + kv[n]
            if finalize:
                o = _finalize(o, oprev_ref, gate_ref, nw_ref, c * CHUNK, h * RET_DV, RET_DV)
            o_ref[rows(c), vcols(h)] = o.astype(o_ref.dtype)
    for h in range(N_HEAD):
        s_ref[h] = s[h]


def _retention(p, cos_tab, sin_tab, norm_w, n_seq, n_lat_blk):
    out = None
    for rev in (True, False):
        finalize = not rev
        consts, a_end = _ret_consts(rev)
        kern = functools.partial(_ret_kernel, rev=rev, finalize=finalize, a_end=a_end)
        out = _scan_call(kern, "retention_bwd" if rev else "retention_fwd", n_seq, n_lat_blk, rev, finalize,
                         [(p, CB_RET_QK), (p, CB_RET_QKROT), (p, CB_RET_V)], [cos_tab, sin_tab],
                         list(consts), [], out, (p, CB_RET_GATE), norm_w,
                         [pltpu.VMEM((N_HEAD, RET_DK, RET_DV), F32)])
    return out


def _rope_tables(seq_len):
    def angles(pos):
        angs = []
        for a, n in enumerate(RET_ROPE_PAIRS):
            freqs = ROPE_BASE ** (-jnp.arange(n, dtype=F32) / n)
            angs.append(pos[:, a:a + 1] * freqs)
        return jnp.concatenate(angs, axis=-1)

    t = jnp.arange(CTX_LEN, dtype=F32)
    zero = jnp.zeros((CTX_LEN,), F32)
    pos_c = jnp.stack([t, zero, zero], axis=-1)
    rows = seq_len // GRID_W
    r = jnp.repeat(jnp.arange(rows, dtype=F32), GRID_W)
    col = jnp.broadcast_to(jnp.arange(GRID_W, dtype=F32), (rows, GRID_W)).reshape(-1)
    pos_l = jnp.stack([jnp.full((seq_len,), CTX_LEN, F32), r, col], axis=-1)
    ang = jnp.concatenate([angles(pos_c), angles(pos_l)], axis=0)
    cos = jnp.concatenate([jnp.cos(ang), jnp.cos(ang)], axis=-1)
    sin = jnp.concatenate([-jnp.sin(ang), jnp.sin(ang)], axis=-1)
    scale = jnp.concatenate([jnp.ones((N_HEAD * RET_DK,), F32),
                             jnp.full((N_HEAD * RET_DK,), RET_DK ** -0.5, F32)])
    return jnp.tile(cos, (1, 2 * N_HEAD)) * scale, jnp.tile(sin, (1, 2 * N_HEAD)) * scale


def _gdn_consts(rev):
    i = np.arange(CHUNK)
    before = _before(rev)
    eye = np.eye(CHUNK, dtype=bool)
    same16 = (i[:, None] // 16) == (i[None, :] // 16)
    same32 = (i[:, None] // 32) == (i[None, :] // 32)
    masks = np.stack([before | eye, before, before & same16, before & same32 & ~same16,
                      before & ~same32, eye]).astype(np.float32)
    return jnp.asarray(masks)


def _unit_tri_inverse(a_all, m_blk, m_mid, m_top, eye):
    both = lambda f, xs, ys: [f(x, y) for x, y in zip(xs, ys)]
    a_blk = [a * m_blk for a in a_all]
    x = [eye + a for a in a_blk]
    pw = both(_dot_f32, a_blk, a_blk)
    for _ in range(2):
        x = both(lambda x_, p_: x_ + _dot_f32(x_, p_), x, pw)
        pw = both(_dot_f32, pw, pw)
    x = both(lambda x_, p_: x_ + _dot_f32(x_, p_), x, pw)
    for m in (m_mid, m_top):
        y = both(lambda x_, a_: _dot_f32(x_, a_ * m), x, a_all)
        x = both(lambda x_, y_: x_ + _dot_f32(y_, x_), x, y)
    return x


def _gdn_kernel(q_ref, k_ref, v_ref, gd_ref, grow_ref, m_ref, *rest, rev, finalize):
    if finalize:
        oprev_ref, gate_ref, nw_ref, o_ref, s_ref = rest
    else:
        o_ref, s_ref = rest

    @pl.when(pl.program_id(1) == 0)
    def _():
        s_ref[...] = jnp.zeros_like(s_ref)

    m_incl, m_strict, m_blk, m_mid, m_top, eye = (m_ref[n] for n in range(6))
    units = [(c, h) for c in _chunk_order(rev) for h in range(N_HEAD)]
    rows = lambda c: slice(c * CHUNK, (c + 1) * CHUNK)
    cols = lambda h: slice(h * GDN_DK, (h + 1) * GDN_DK)

    q = [q_ref[rows(c), cols(h)] for c, h in units]
    k = [k_ref[rows(c), cols(h)] for c, h in units]
    v = [v_ref[rows(c), cols(h)] for c, h in units]
    g_col = [gd_ref[rows(c), h:h + 1] for c, h in units]
    beta = [gd_ref[rows(c), N_HEAD + h:N_HEAD + h + 1] for c, h in units]
    g_row = [grow_ref[c, h:h + 1, :] for c, h in units]
    g_tot = [g[0:1] if rev else g[CHUNK - 1:CHUNK] for g in g_col]
    decay = [jnp.exp(jnp.where(m_incl > 0, gc - gr, -jnp.inf)) for gc, gr in zip(g_col, g_row)]
    k16 = [t.astype(BF16) for t in k]
    k_beta = [t * b for t, b in zip(k, beta)]
    a = [-(_dot_nt(kb.astype(BF16), kk) * d) * m_strict for kb, kk, d in zip(k_beta, k16, decay)]
    t16 = [(x - eye).astype(BF16) for x in _unit_tri_inverse(a, m_blk, m_mid, m_top, eye)]
    v_beta = [t * b for t, b in zip(v, beta)]
    e_col = [jnp.exp(g) for g in g_col]
    kbg = [kb * e for kb, e in zip(k_beta, e_col)]
    u = [vb + _dot(t, vb.astype(BF16)) for t, vb in zip(t16, v_beta)]
    w16 = [(x + _dot(t, x.astype(BF16))).astype(BF16) for t, x in zip(t16, kbg)]
    sc16 = [(_dot_nt(qq.astype(BF16), kk) * d).astype(BF16) for qq, kk, d in zip(q, k16, decay)]
    q_in = [(qq * e).astype(BF16) for qq, e in zip(q, e_col)]
    k_out = [(kk * jnp.exp(gt - gc)).astype(BF16) for kk, gt, gc in zip(k, g_tot, g_col)]
    a_end = [jnp.exp(gt) for gt in g_tot]

    s = [s_ref[h] for h in range(N_HEAD)]
    for ci in range(CPB):
        us = range(ci * N_HEAD, (ci + 1) * N_HEAD)
        s16 = [t.astype(BF16) for t in s]
        vn16 = [(u[n] - _dot(w16[n], s16[h])).astype(BF16) for h, n in enumerate(us)]
        o = [_dot(q_in[n], s16[h]) + _dot(sc16[n], vn16[h]) for h, n in enumerate(us)]
        s = [a_end[n] * s[h] + _dot_tn(k_out[n], vn16[h]) for h, n in enumerate(us)]
        for h, n in enumerate(us):
            c = units[n][0]
            oh = o[h]
            if finalize:
                oh = _finalize(oh, oprev_ref, gate_ref, nw_ref, c * CHUNK, h * GDN_DV, GDN_DV)
            o_ref[rows(c), cols(h)] = oh.astype(o_ref.dtype)
    for h in range(N_HEAD):
        s_ref[h] = s[h]


def _conv3(x, w):
    xp = jnp.pad(x, ((0, 0), (1, 1), (0, 0)))
    return xp[:, :-2] * w[0] + xp[:, 1:-1] * w[1] + xp[:, 2:] * w[2]


def _per_seq(fn, t, n_seq, n_ctx_rows):
    C = t.shape[-1]
    yc = fn(t[:n_ctx_rows].reshape(n_seq, -1, C))
    yl = fn(t[n_ctx_rows:].reshape(n_seq, -1, C))
    return jnp.concatenate([yc.reshape(-1, yc.shape[-1]), yl.reshape(-1, yl.shape[-1])], axis=0)


def _gdn(p, p_small, conv_w, a_log, dt_bias, norm_w, n_seq, n_lat_blk):
    T = p.shape[0]
    n_ctx_rows = n_seq * CTX_LEN

    def prep(z):
        z = _silu(_conv3(z, conv_w))
        q, k, v = jnp.split(z, 3, axis=-1)
        hn = lambda t: t.reshape(*t.shape[:-1], N_HEAD, GDN_DK)
        l2 = lambda t: (t * lax.rsqrt(jnp.sum(t * t, axis=-1, keepdims=True) + EPS)).reshape(*z.shape[:-1], -1)
        return jnp.concatenate([l2(hn(q)) * GDN_DK ** -0.5, l2(hn(k)), v], axis=-1)

    qkv = _per_seq(prep, p[:, CB_GDN_Q * COLW:(CB_GDN_Q + 3) * COLW], n_seq, n_ctx_rows)
    a = p_small[:, SM_A:SM_A + 2 * N_HEAD].reshape(T, 2, N_HEAD)
    b = p_small[:, SM_B:SM_B + 2 * N_HEAD].reshape(T, 2, N_HEAD)
    g = (-jnp.exp(a_log) * jax.nn.softplus(a + dt_bias)).reshape(T // CHUNK, CHUNK, 2, N_HEAD)
    beta = jax.nn.sigmoid(b)
    out = None
    for rev in (True, False):
        finalize = not rev
        d = 1 if rev else 0
        gd_ = g[:, :, d]
        G = jnp.flip(jnp.cumsum(jnp.flip(gd_, 1), axis=1), 1) if rev else jnp.cumsum(gd_, axis=1)
        gd = jnp.concatenate([G.reshape(T, N_HEAD), beta[:, d]], axis=-1)
        grow = jnp.transpose(G, (0, 2, 1))
        rows = _scan_rows(n_seq, n_lat_blk, rev)
        extra = [(gd, pl.BlockSpec((SCAN_BLK, 2 * N_HEAD), lambda b_, j, rows=rows: (rows(b_, j), 0))),
                 (grow, pl.BlockSpec((CPB, N_HEAD, CHUNK), lambda b_, j, rows=rows: (rows(b_, j), 0, 0)))]
        kern = functools.partial(_gdn_kernel, rev=rev, finalize=finalize)
        out = _scan_call(kern, "gdn_bwd" if rev else "gdn_fwd", n_seq, n_lat_blk, rev, finalize,
                         [(qkv, 0), (qkv, 1), (qkv, 2)], [], [_gdn_consts(rev)], extra,
                         out, (p, CB_GDN_GATE), norm_w,
                         [pltpu.VMEM((N_HEAD, GDN_DK, GDN_DV), F32)])
    return out


GLA_SUB = 4


def _gla_consts(rev):
    C = CHUNK
    idx = np.arange(C)
    before = _before(rev)
    incl = before | np.eye(C, dtype=bool)
    sizes = (16, 4, 1)
    blk = [idx // s for s in sizes]
    par = [idx // 64, idx // 16, idx // 4]
    sub = [b % GLA_SUB for b in blk]

    def blk_before(l, a, b):
        return (blk[l][a] > blk[l][b]) if rev else (blk[l][a] < blk[l][b])

    pq = np.zeros((2, C, C), np.float32)
    for l in range(2):
        pq[l] = incl & (blk[l][:, None] == blk[l][None, :])
    kmt = np.zeros((3, C, GLA_SUB * C), np.float32)
    maskx = np.zeros((4, C, GLA_SUB * C), np.float32)
    for l in range(3):
        for m in range(GLA_SUB):
            for j in range(C):
                tgt = par[l][j] * GLA_SUB + m
                valid = (tgt < blk[l][j]) if rev else (tgt > blk[l][j])
                if not valid:
                    continue
                t_after_j = before[:, j]
                t_blk_before_tgt = (blk[l] > tgt) if rev else (blk[l] < tgt)
                kmt[l, :, m * C + j] = t_after_j & t_blk_before_tgt
            for i in range(C):
                if sub[l][i] != m:
                    continue
                sel = (par[l] == par[l][i]) & np.array([blk_before(l, j, i) for j in range(C)])
                maskx[l, i, m * C:(m + 1) * C] = sel
    maskx[3, :, :C] = np.eye(C)
    id4 = np.tile(np.eye(C, dtype=np.float32), (1, GLA_SUB))
    return (jnp.asarray(incl.astype(np.float32), BF16), jnp.asarray(pq, BF16), jnp.asarray(kmt, BF16),
            jnp.asarray(maskx, F32), jnp.asarray(id4, BF16))


def _gla_kernel(qk_ref, v_ref, sm_ref, w2_ref, b2_ref, incl_ref, pq_ref, kmt_ref, mx_ref, id4_ref, *rest,
                rev, finalize):
    if finalize:
        oprev_ref, gate_ref, nw_ref, o_ref, st_ref = rest
    else:
        o_ref, st_ref = rest

    @pl.when(pl.program_id(1) == 0)
    def _():
        st_ref[...] = jnp.zeros_like(st_ref)

    hw = N_HEAD * GLA_DK
    x = _dot(sm_ref[...].astype(BF16), w2_ref[...]) + b2_ref[...]
    gk_all = (jnp.minimum(x, 0.0) - jnp.log(1.0 + jnp.exp(-jnp.abs(x)))) * (1.0 / GLA_TAU)
    incl = incl_ref[...]
    id4 = id4_ref[...]
    order = list(_chunk_order(rev))
    units = [(ci, h) for ci in range(CPB) for h in range(N_HEAD)]
    rows = lambda c: slice(c * CHUNK, (c + 1) * CHUNK)
    kcols = lambda h: slice(h * GLA_DK, (h + 1) * GLA_DK)
    vcols = lambda h: slice(h * GLA_DV, (h + 1) * GLA_DV)

    gk = [gk_all[rows(c)] for c in order]
    q = [qk_ref[rows(c), :hw] * GLA_DK ** -0.5 for c in order]
    k = [qk_ref[rows(c), hw:] for c in order]
    vb = [v_ref[rows(c), :].astype(BF16) for c in order]
    G = [_dot_const(incl, g, 3) for g in gk]
    g_tot = [g[0:1] if rev else g[CHUNK - 1:CHUNK] for g in G]
    e1 = [_dot_const(pq_ref[0], g, 3) for g in gk]
    e2 = [_dot_const(pq_ref[1], g, 3) for g in gk]
    q_lvl = [[(qq * jnp.exp(e)).astype(BF16) for qq, e in zip(q, es)] for es in (e1, e2, gk)]
    q_lvl.append([qq.astype(BF16) for qq in q])
    q_in = [(qq * jnp.exp(g)).astype(BF16) for qq, g in zip(q, G)]
    k_out = [(kk * jnp.exp(gt - g)).astype(BF16) for kk, gt, g in zip(k, g_tot, G)]
    kb = [kk.astype(BF16) for kk in k]
    decay_tot = [jnp.exp(gt) for gt in g_tot]

    kt = [_dot_tn(kb[ci][:, kcols(h)], id4) for ci, h in units]
    sx = [_dot(q_lvl[3][ci][:, kcols(h)], t.astype(BF16)) * mx_ref[3] for (ci, h), t in zip(units, kt)]
    for l in range(3):
        e = [_dot_tn_const(gk[ci][:, kcols(h)], kmt_ref[l], 2) for ci, h in units]
        kt_l = [(t * jnp.exp(e_)).astype(BF16) for t, e_ in zip(kt, e)]
        sx = [s_ + _dot(q_lvl[l][ci][:, kcols(h)], t) * mx_ref[l] for (ci, h), s_, t in zip(units, sx, kt_l)]
    vh = [vb[ci][:, vcols(h)] for ci, h in units]
    o_intra = [_dot(s_.astype(BF16), jnp.concatenate([v_] * GLA_SUB, axis=0)) for s_, v_ in zip(sx, vh)]
    kv = [_dot_tn(v_, k_out[ci][:, kcols(h)]) for (ci, h), v_ in zip(units, vh)]

    st = [st_ref[h] for h in range(N_HEAD)]
    for ci in range(CPB):
        c = order[ci]
        for h in range(N_HEAD):
            n = ci * N_HEAD + h
            o = o_intra[n] + _dot_nt(q_in[ci][:, kcols(h)], st[h].astype(BF16))
            st[h] = st[h] * decay_tot[ci][:, kcols(h)] + kv[n]
            if finalize:
                o = _finalize(o, oprev_ref, gate_ref, nw_ref, c * CHUNK, h * GLA_DV, GLA_DV)
            o_ref[rows(c), vcols(h)] = o.astype(o_ref.dtype)
    for h in range(N_HEAD):
        st_ref[h] = st[h]


def _gla(p, p_small, w2, b2, norm_w, n_seq, n_lat_blk):
    out = None
    hw = N_HEAD * GLA_DK
    for rev in (True, False):
        finalize = not rev
        d = 1 if rev else 0
        lr0 = SM_LR + d * GLA_RANK
        w2p = jnp.zeros((V7X_LANES, hw), F32).at[lr0:lr0 + GLA_RANK].set(w2[d]).astype(BF16)
        rows = _scan_rows(n_seq, n_lat_blk, rev)
        extra = [(p_small, pl.BlockSpec((SCAN_BLK, V7X_LANES), lambda b_, j, rows=rows: (rows(b_, j), 0)))]
        kern = functools.partial(_gla_kernel, rev=rev, finalize=finalize)
        out = _scan_call(kern, "gla_bwd" if rev else "gla_fwd", n_seq, n_lat_blk, rev, finalize,
                         [(p, CB_GLA_QK), (p, CB_GLA_V)], [], [w2p, b2[d].reshape(1, hw), *_gla_consts(rev)],
                         extra, out, (p, CB_GLA_GATE), norm_w,
                         [pltpu.VMEM((N_HEAD, GLA_DV, GLA_DK), F32)])
    return out


def _hyena_filters(L, w1, b1, w2, b2, w3, freq):
    t = jnp.linspace(0.0, 1.0, L, dtype=F32)[:, None]
    w = 2.0 * math.pi * jnp.arange(L, dtype=F32) / L
    f = jnp.linspace(1e-4, HY_BANDS - 1, HY_BANDS, dtype=F32)
    ang = w[:, None] * f[None, :]
    z = jnp.concatenate([t, jnp.cos(ang), -jnp.sin(ang)], axis=-1)
    hdn = jnp.sin(freq[0] * (z @ w1 + b1))
    for i in range(HY_INNER):
        hdn = jnp.sin(freq[i + 1] * (hdn @ w2[i] + b2[i]))
    h = (hdn @ w3).reshape(L, HY_ORDER, 2, HY_W)
    deltas = jnp.abs(jnp.linspace(HY_MIN_DECAY, HY_MAX_DECAY, HY_W, dtype=F32))
    h = h * jnp.exp(-t[:, :, None, None] * deltas)
    hf, hb = h[:, :, 0], h[:, :, 1]
    k2 = jnp.concatenate([hf, jnp.zeros_like(hf[:1]), jnp.flip(hb[1:], axis=0)], axis=0)
    k2 = k2 * lax.rsqrt(jnp.sum(k2 * k2, axis=0, keepdims=True) + EPS)
    return jnp.fft.rfft(k2, axis=0)


def _fft_long_conv(z, kf):
    L = z.shape[1]
    zf = jnp.fft.rfft(z, n=2 * L, axis=1)
    return jnp.fft.irfft(zf * kf[None], n=2 * L, axis=1)[:, :L]


def _hyena(p, conv_w, conv_b, w1, b1, w2, b2, w3, freq, skip, n_seq):
    def mix(zg):
        z, gate = zg[..., :3 * HY_W], zg[..., 3 * HY_W:]
        L = z.shape[1]
        z = _conv3(z, conv_w) + conv_b
        v, x1, x2 = jnp.split(z, 3, axis=-1)
        kf = _hyena_filters(L, w1, b1, w2, b2, w3, freq)
        y = x1 * (_fft_long_conv(v, kf[:, 0]) + skip[0] * v)
        y = x2 * (_fft_long_conv(y, kf[:, 1]) + skip[1] * y)
        return (y * _silu(gate)).astype(BF16)

    return _per_seq(mix, p[:, :(CB_HY_GATE + 1) * COLW], n_seq, n_seq * CTX_LEN)


def _layout_w_in(w_in):
    sizes = [3 * HY_W, HY_W, 3 * BR_W, 2 * N_HEAD, 2 * N_HEAD, BR_W, 2 * N_HEAD * RET_DK, BR_W, BR_W,
             2 * N_HEAD * GLA_DK, BR_W, 2 * GLA_RANK, BR_W]
    offs = np.concatenate([[0], np.cumsum(sizes)])
    (hy_proj, hy_gate, gdn_qkv, gdn_a, gdn_b, gdn_gate, ret_qk, ret_v, ret_gate,
     gla_qk, gla_v, gla_lr, gla_gate) = [w_in[:, offs[n]:offs[n + 1]] for n in range(len(sizes))]
    D = w_in.shape[0]
    half = RET_DK // 2
    ret_rot = ret_qk.reshape(D, 2 * N_HEAD, 2, half)[:, :, ::-1].reshape(D, -1)
    main = jnp.concatenate([hy_proj, hy_gate, gdn_qkv, gdn_gate, ret_qk, ret_rot, ret_v, ret_gate,
                            gla_qk, gla_v, gla_gate], axis=1)
    small = jnp.concatenate([gdn_a, gdn_b, gla_lr], axis=1)
    small = jnp.pad(small, ((0, 0), (0, V7X_LANES - small.shape[1])))
    return main, small


def _layer(x, mod, dims, rope, norm_w, w_in, hy_conv_w, hy_conv_b, hy_w1, hy_b1, hy_w2, hy_b2, hy_w3,
           hy_freq, hy_skip, gdn_conv_w, gdn_a_log, gdn_dt_bias, gdn_norm, ret_norm, gla_w2, gla_b2,
           gla_norm, w_branch, w_merge, b_merge, w_out):
    n_seq, seq_len = dims
    n_ctx_rows = n_seq * CTX_LEN
    n_lat_blk = seq_len // SCAN_BLK
    shift, scale, gate = mod
    w_main, w_small = _layout_w_in(w_in.astype(BF16))
    p, h = _inproj(x, norm_w, scale, shift, w_main, n_ctx_rows, seq_len)
    p_small = _matmul(h, w_small)

    y_hy = _hyena(p, hy_conv_w, hy_conv_b, hy_w1, hy_b1, hy_w2, hy_b2, hy_w3, hy_freq, hy_skip, n_seq)
    y_gdn = _gdn(p, p_small, gdn_conv_w, gdn_a_log, gdn_dt_bias, gdn_norm, n_seq, n_lat_blk)
    y_ret = _retention(p, rope[0], rope[1], ret_norm, n_seq, n_lat_blk)
    y_gla = _gla(p, p_small, gla_w2, gla_b2, gla_norm, n_seq, n_lat_blk)

    merged = _merge(h, (y_hy, y_gdn, y_ret, y_gla), w_merge.astype(BF16), b_merge, w_branch.astype(BF16))
    return _outproj(merged, w_out.astype(BF16), x, gate, n_ctx_rows, seq_len)


def kernel(x, c, ctx, c_ctx, norm_w, ada_w, ada_b, w_in, hy_conv_w, hy_conv_b, hy_w1, hy_b1,
           hy_w2, hy_b2, hy_w3, hy_freq, hy_skip, gdn_conv_w, gdn_a_log, gdn_dt_bias,
           gdn_norm, ret_norm, gla_w2, gla_b2, gla_norm, w_branch, w_merge, b_merge, w_out,
           final_norm):
    B, S, D = x.shape
    layer_params = (norm_w, w_in, hy_conv_w, hy_conv_b, hy_w1, hy_b1, hy_w2, hy_b2, hy_w3, hy_freq,
                    hy_skip, gdn_conv_w, gdn_a_log, gdn_dt_bias, gdn_norm, ret_norm, gla_w2, gla_b2,
                    gla_norm, w_branch, w_merge, b_merge, w_out)
    xt = jnp.concatenate([ctx.reshape(-1, D), x.reshape(-1, D)], axis=0)
    cond = _silu(jnp.concatenate([c_ctx[None], c], axis=0))
    cond = jnp.pad(cond, ((0, -(1 + B) % 8), (0, 0))).astype(BF16)
    rope = _rope_tables(S)
    for l in range(DEPTH):
        m = (_matmul(cond, ada_w[l].astype(BF16)) + ada_b[l])[:1 + B]
        mod = tuple(t.reshape(1 + B, 1, D) for t in jnp.split(m, 3, axis=-1))
        xt = _layer(xt, mod, (B, S), rope, *[p[l] for p in layer_params])
    return _final_norm(xt, final_norm, B * CTX_LEN).reshape(B, S, D)
```

```python
import functools
import math

import numpy as np
import jax
import jax.numpy as jnp
from jax import lax
from jax.experimental import pallas as pl
from jax.experimental.pallas import tpu as pltpu

D_MODEL = 2048
DEPTH = 4
CTX_LEN = 256
GRID_W = 64
F32 = jnp.float32
BF16 = jnp.bfloat16
EPS = 1e-6
CHUNK = 64

N_BRANCH = 4
BR_W = D_MODEL // N_BRANCH

HY_W = BR_W
HY_ORDER = 2
HY_EMB = 33
HY_BANDS = (HY_EMB - 1) // 2
HY_INNER = 2
HY_FAST_DECAY = 0.3
HY_SLOW_DECAY = 1.5
HY_TARGET = 1e-2
HY_MIN_DECAY = math.log(HY_TARGET) / HY_SLOW_DECAY
HY_MAX_DECAY = math.log(HY_TARGET) / HY_FAST_DECAY

N_HEAD = 4
GDN_DK = BR_W // N_HEAD
GDN_DV = BR_W // N_HEAD
RET_DK = BR_W // (2 * N_HEAD)
RET_DV = BR_W // N_HEAD
RET_ROPE_PAIRS = (8, 12, 12)
ROPE_BASE = 10000.0
GLA_DK = BR_W // (2 * N_HEAD)
GLA_DV = BR_W // N_HEAD
GLA_RANK = 16
GLA_TAU = 16.0

V7X_LANES = 128
VMEM_LIMIT_BYTES = 56 * 1024 * 1024

SCAN_BLK = CTX_LEN
CPB = SCAN_BLK // CHUNK
COLW = BR_W

CB_HY_V, CB_HY_X1, CB_HY_X2, CB_HY_GATE = 0, 1, 2, 3
CB_GDN_Q, CB_GDN_GATE = 4, 7
CB_RET_QK, CB_RET_QKROT, CB_RET_V, CB_RET_GATE = 8, 9, 10, 11
CB_GLA_QK, CB_GLA_V, CB_GLA_GATE = 12, 13, 14
N_COLBLK = 15
SM_A, SM_B, SM_LR = 0, 2 * N_HEAD, 4 * N_HEAD


def _dot(a, b):
    return jnp.dot(a, b, preferred_element_type=F32)


def _dot_nt(a, b):
    return lax.dot_general(a, b, (((1,), (1,)), ((), ())), preferred_element_type=F32)


def _dot_tn(a, b):
    return lax.dot_general(a, b, (((0,), (0,)), ((), ())), preferred_element_type=F32)


def _split_bf16(x, terms):
    parts = []
    r = x
    for t in range(terms):
        p = r.astype(BF16)
        parts.append(p)
        if t + 1 < terms:
            r = r - p.astype(F32)
    return parts


def _dot_f32(a, b):
    a1, a2 = _split_bf16(a, 2)
    b1, b2 = _split_bf16(b, 2)
    return _dot(a1, b1) + (_dot(a1, b2) + _dot(a2, b1))


def _dot_const(c, x, terms):
    out = None
    for p in _split_bf16(x, terms):
        t = _dot(c, p)
        out = t if out is None else out + t
    return out


def _dot_tn_const(x, c, terms):
    out = None
    for p in _split_bf16(x, terms):
        t = _dot_tn(p, c)
        out = t if out is None else out + t
    return out


def _silu(x):
    return x * jax.nn.sigmoid(x)


def _cparams(sem):
    return pltpu.CompilerParams(dimension_semantics=sem, vmem_limit_bytes=VMEM_LIMIT_BYTES)


def _mm_kernel(a_ref, b_ref, o_ref):
    o_ref[...] = _dot(a_ref[...], b_ref[...]).astype(o_ref.dtype)


def _pick_tile(n, cands):
    for c in cands:
        if n % c == 0:
            return c
    return n


def _matmul(a, b, out_dtype=F32):
    M, K = a.shape
    _, N = b.shape
    tm = _pick_tile(M, (1024, 512, 256, 128, 64, 32, 16, 8))
    tn = _pick_tile(N, (512, 384, 256, 128))
    return pl.pallas_call(
        _mm_kernel,
        grid=(M // tm, N // tn),
        in_specs=[pl.BlockSpec((tm, K), lambda i, j: (i, 0)),
                  pl.BlockSpec((K, tn), lambda i, j: (0, j))],
        out_specs=pl.BlockSpec((tm, tn), lambda i, j: (i, j)),
        out_shape=jax.ShapeDtypeStruct((M, N), out_dtype),
        compiler_params=_cparams(("parallel", "arbitrary")),
        name="matmul",
    )(a, b)


def _mod_row(n_lat_blk, blk_per_seq):
    return lambda i: jnp.where(i < n_lat_blk, 1 + i // blk_per_seq, 0)


def _inproj_kernel(x_ref, nw_ref, sc_ref, sh_ref, w_ref, p_ref, h_ref, hs_ref):
    @pl.when(pl.program_id(1) == 0)
    def _():
        x = x_ref[...]
        y = x * lax.rsqrt(jnp.mean(x * x, axis=-1, keepdims=True) + EPS) * nw_ref[...]
        hs_ref[...] = (y * (1.0 + sc_ref[0]) + sh_ref[0]).astype(BF16)
        h_ref[...] = hs_ref[...]

    p_ref[...] = _dot(hs_ref[...], w_ref[...])


def _inproj(x, norm_w, scale, shift, w, n_lat_rows, seq_len):
    T, D = x.shape
    N = w.shape[1]
    tm, tn = 512, COLW
    row = _mod_row(n_lat_rows // tm, seq_len // tm)
    return pl.pallas_call(
        _inproj_kernel,
        grid=(T // tm, N // tn),
        in_specs=[pl.BlockSpec((tm, D), lambda i, j: (i, 0)),
                  pl.BlockSpec((1, D), lambda i, j: (0, 0)),
                  pl.BlockSpec((1, 1, D), lambda i, j: (row(i), 0, 0)),
                  pl.BlockSpec((1, 1, D), lambda i, j: (row(i), 0, 0)),
                  pl.BlockSpec((D, tn), lambda i, j: (0, j))],
        out_specs=[pl.BlockSpec((tm, tn), lambda i, j: (i, j)),
                   pl.BlockSpec((tm, D), lambda i, j: (i, 0))],
        out_shape=[jax.ShapeDtypeStruct((T, N), F32), jax.ShapeDtypeStruct((T, D), BF16)],
        scratch_shapes=[pltpu.VMEM((tm, D), BF16)],
        compiler_params=_cparams(("parallel", "arbitrary")),
        name="inproj",
    )(x, norm_w.reshape(1, D), scale, shift, w)


def _merge_kernel(h_ref, y0_ref, y1_ref, y2_ref, y3_ref, wm_ref, bm_ref, wb_ref, o_ref):
    h = h_ref[...]
    acc = None
    for i, y_ref in enumerate((y0_ref, y1_ref, y2_ref, y3_ref)):
        g = jax.nn.sigmoid(_dot(h, wm_ref[i]) + bm_ref[i])
        t = g * _dot(y_ref[...], wb_ref[i])
        acc = t if acc is None else acc + t
    o_ref[...] = acc.astype(o_ref.dtype)


def _merge(h, ys, w_merge, b_merge, w_branch):
    T, D = h.shape
    tm, tn = 1024, 256
    yspec = pl.BlockSpec((tm, BR_W), lambda i, j: (i, 0))
    return pl.pallas_call(
        _merge_kernel,
        grid=(T // tm, D // tn),
        in_specs=[pl.BlockSpec((tm, D), lambda i, j: (i, 0)), yspec, yspec, yspec, yspec,
                  pl.BlockSpec((N_BRANCH, D, tn), lambda i, j: (0, 0, j)),
                  pl.BlockSpec((N_BRANCH, 1, tn), lambda i, j: (0, 0, j)),
                  pl.BlockSpec((N_BRANCH, BR_W, tn), lambda i, j: (0, 0, j))],
        out_specs=pl.BlockSpec((tm, tn), lambda i, j: (i, j)),
        out_shape=jax.ShapeDtypeStruct((T, D), BF16),
        compiler_params=_cparams(("parallel", "arbitrary")),
        name="merge",
    )(h, *ys, w_merge, b_merge.reshape(N_BRANCH, 1, D), w_branch)


def _outproj_kernel(a_ref, w_ref, x_ref, g_ref, o_ref):
    o_ref[...] = x_ref[...] + g_ref[0] * _dot(a_ref[...], w_ref[...])


def _outproj(a, w, x, gate, n_lat_rows, seq_len):
    T, D = x.shape
    tm, tn = 1024, 512
    row = _mod_row(n_lat_rows // tm, seq_len // tm)
    return pl.pallas_call(
        _outproj_kernel,
        grid=(T // tm, D // tn),
        in_specs=[pl.BlockSpec((tm, D), lambda i, j: (i, 0)),
                  pl.BlockSpec((D, tn), lambda i, j: (0, j)),
                  pl.BlockSpec((tm, tn), lambda i, j: (i, j)),
                  pl.BlockSpec((1, 1, tn), lambda i, j: (row(i), 0, j))],
        out_specs=pl.BlockSpec((tm, tn), lambda i, j: (i, j)),
        out_shape=jax.ShapeDtypeStruct((T, D), F32),
        compiler_params=_cparams(("parallel", "arbitrary")),
        name="outproj",
    )(a, w, x, gate)


def _final_norm_kernel(x_ref, w_ref, o_ref):
    x = x_ref[...]
    o_ref[...] = x * lax.rsqrt(jnp.mean(x * x, axis=-1, keepdims=True) + EPS) * w_ref[...]


def _final_norm(x, w, n_lat_rows):
    D = x.shape[1]
    tm = 512
    return pl.pallas_call(
        _final_norm_kernel,
        grid=(n_lat_rows // tm,),
        in_specs=[pl.BlockSpec((tm, D), lambda i: (i, 0)),
                  pl.BlockSpec((1, D), lambda i: (0, 0))],
        out_specs=pl.BlockSpec((tm, D), lambda i: (i, 0)),
        out_shape=jax.ShapeDtypeStruct((n_lat_rows, D), F32),
        compiler_params=_cparams(("parallel",)),
        name="final_norm",
    )(x, w.reshape(1, D))


def _scan_rows(n_seq, n_lat_blk, rev):
    def idx(b, j):
        jj = (n_lat_blk - j) if rev else (j - 1)
        return jnp.where(j == 0, n_seq * n_lat_blk + b, b * n_lat_blk + jj)
    return idx


def _chunk_order(rev):
    return range(CPB - 1, -1, -1) if rev else range(CPB)


def _before(rev):
    i = np.arange(CHUNK)[:, None]
    j = np.arange(CHUNK)[None, :]
    return (j > i) if rev else (j < i)


def _finalize(o, oprev_ref, gate_ref, nw_ref, r0, c0, width):
    o = o + oprev_ref[r0:r0 + CHUNK, c0:c0 + width]
    o = o * lax.rsqrt(jnp.mean(o * o, axis=-1, keepdims=True) + EPS) * nw_ref[...]
    return o * _silu(gate_ref[r0:r0 + CHUNK, c0:c0 + width])


def _scan_call(kernel, name, n_seq, n_lat_blk, rev, finalize, row_inputs, tab_inputs, const_inputs,
               extra_specs_inputs, oprev, gate_src, norm_w, scratch):
    rows = _scan_rows(n_seq, n_lat_blk, rev)
    T = row_inputs[0][0].shape[0]
    args, specs = [], []
    for arr, cb in row_inputs:
        args.append(arr)
        specs.append(pl.BlockSpec((SCAN_BLK, COLW), functools.partial(
            lambda b, j, cb: (rows(b, j), cb), cb=cb)))
    for arr in tab_inputs:
        args.append(arr)
        specs.append(pl.BlockSpec((SCAN_BLK, arr.shape[1]), lambda b, j: (
            jnp.where(j == 0, 0, 1 + ((n_lat_blk - j) if rev else (j - 1))), 0)))
    for arr, spec in extra_specs_inputs:
        args.append(arr)
        specs.append(spec)
    for arr in const_inputs:
        args.append(arr)
        specs.append(pl.BlockSpec(arr.shape, functools.partial(lambda b, j, n: (0,) * n, n=arr.ndim)))
    if finalize:
        args += [oprev, gate_src[0], norm_w.reshape(1, -1)]
        specs += [pl.BlockSpec((SCAN_BLK, COLW), lambda b, j: (rows(b, j), 0)),
                  pl.BlockSpec((SCAN_BLK, COLW), functools.partial(
                      lambda b, j, cb: (rows(b, j), cb), cb=gate_src[1])),
                  pl.BlockSpec((1, norm_w.shape[0]), lambda b, j: (0, 0))]
    return pl.pallas_call(
        kernel,
        grid=(n_seq, 1 + n_lat_blk),
        in_specs=specs,
        out_specs=pl.BlockSpec((SCAN_BLK, COLW), lambda b, j: (rows(b, j), 0)),
        out_shape=jax.ShapeDtypeStruct((T, COLW), BF16 if finalize else F32),
        scratch_shapes=scratch,
        compiler_params=_cparams(("parallel", "arbitrary")),
        name=name,
    )(*args)


def _ret_consts(rev):
    lg = np.log(1.0 - np.power(2.0, -5.0 - np.arange(N_HEAD, dtype=np.float64)))
    i = np.arange(CHUNK, dtype=np.float64)
    steps = (CHUNK - i) if rev else (i + 1.0)
    G = steps[None, :] * lg[:, None]
    mask = _before(rev) | np.eye(CHUNK, dtype=bool)
    dm = np.where(mask[None], np.exp(G[:, :, None] - G[:, None, :]), 0.0)
    qs = np.repeat(np.exp(G).T, RET_DK, axis=1)
    ks = np.repeat(np.exp(CHUNK * lg[:, None] - G).T, RET_DK, axis=1)
    a_end = tuple(float(v) for v in np.exp(CHUNK * lg))
    return (jnp.asarray(dm, F32), jnp.asarray(qs, F32), jnp.asarray(ks, F32)), a_end


def _ret_kernel(qk_ref, qkr_ref, v_ref, cos_ref, sin_ref, dm_ref, qs_ref, ks_ref, *rest,
                rev, finalize, a_end):
    if finalize:
        oprev_ref, gate_ref, nw_ref, o_ref, s_ref = rest
    else:
        o_ref, s_ref = rest

    @pl.when(pl.program_id(1) == 0)
    def _():
        s_ref[...] = jnp.zeros_like(s_ref)

    hw = N_HEAD * RET_DK
    qk = qk_ref[...] * cos_ref[...] + qkr_ref[...] * sin_ref[...]
    order = list(_chunk_order(rev))
    units = [(ci, h) for ci in range(CPB) for h in range(N_HEAD)]
    rows = lambda c: slice(c * CHUNK, (c + 1) * CHUNK)
    kcols = lambda h: slice(h * RET_DK, (h + 1) * RET_DK)
    vcols = lambda h: slice(h * RET_DV, (h + 1) * RET_DV)

    qc = [qk[rows(c), :hw] for c in order]
    kc = [qk[rows(c), hw:] for c in order]
    qb = [t.astype(BF16) for t in qc]
    kb = [t.astype(BF16) for t in kc]
    q_in = [(t * qs_ref[...]).astype(BF16) for t in qc]
    k_out = [(t * ks_ref[...]).astype(BF16) for t in kc]
    vb = [v_ref[rows(c), :].astype(BF16) for c in order]
    vh = [vb[ci][:, vcols(h)] for ci, h in units]
    sc = [(_dot_nt(qb[ci][:, kcols(h)], kb[ci][:, kcols(h)]) * dm_ref[h]).astype(BF16) for ci, h in units]
    o_intra = [_dot(s_, v_) for s_, v_ in zip(sc, vh)]
    kv = [_dot_tn(k_out[ci][:, kcols(h)], v_) for (ci, h), v_ in zip(units, vh)]

    s = [s_ref[h] for h in range(N_HEAD)]
    for ci in range(CPB):
        c = order[ci]
        for h in range(N_HEAD):
            n = ci * N_HEAD + h
            o = o_intra[n] + _dot(q_in[ci][:, kcols(h)], s[h].astype(BF16))
            s[h] = a_end[h] * s[h] + kv[n]
            if finalize:
                o = _finalize(o, oprev_ref, gate_ref, nw_ref, c * CHUNK, h * RET_DV, RET_DV)
            o_ref[rows(c), vcols(h)] = o.astype(o_ref.dtype)
    for h in range(N_HEAD):
        s_ref[h] = s[h]


def _retention(p, cos_tab, sin_tab, norm_w, n_seq, n_lat_blk):
    out = None
    for rev in (True, False):
        finalize = not rev
        consts, a_end = _ret_consts(rev)
        kern = functools.partial(_ret_kernel, rev=rev, finalize=finalize, a_end=a_end)
        out = _scan_call(kern, "retention_bwd" if rev else "retention_fwd", n_seq, n_lat_blk, rev, finalize,
                         [(p, CB_RET_QK), (p, CB_RET_QKROT), (p, CB_RET_V)], [cos_tab, sin_tab],
                         list(consts), [], out, (p, CB_RET_GATE), norm_w,
                         [pltpu.VMEM((N_HEAD, RET_DK, RET_DV), F32)])
    return out


def _rope_tables(seq_len):
    def angles(pos):
        angs = []
        for a, n in enumerate(RET_ROPE_PAIRS):
            freqs = ROPE_BASE ** (-jnp.arange(n, dtype=F32) / n)
            angs.append(pos[:, a:a + 1] * freqs)
        return jnp.concatenate(angs, axis=-1)

    t = jnp.arange(CTX_LEN, dtype=F32)
    zero = jnp.zeros((CTX_LEN,), F32)
    pos_c = jnp.stack([t, zero, zero], axis=-1)
    rows = seq_len // GRID_W
    r = jnp.repeat(jnp.arange(rows, dtype=F32), GRID_W)
    col = jnp.broadcast_to(jnp.arange(GRID_W, dtype=F32), (rows, GRID_W)).reshape(-1)
    pos_l = jnp.stack([jnp.full((seq_len,), CTX_LEN, F32), r, col], axis=-1)
    ang = jnp.concatenate([angles(pos_c), angles(pos_l)], axis=0)
    cos = jnp.concatenate([jnp.cos(ang), jnp.cos(ang)], axis=-1)
    sin = jnp.concatenate([-jnp.sin(ang), jnp.sin(ang)], axis=-1)
    scale = jnp.concatenate([jnp.ones((N_HEAD * RET_DK,), F32),
                             jnp.full((N_HEAD * RET_DK,), RET_DK ** -0.5, F32)])
    return jnp.tile(cos, (1, 2 * N_HEAD)) * scale, jnp.tile(sin, (1, 2 * N_HEAD)) * scale


def _gdn_consts(rev):
    i = np.arange(CHUNK)
    before = _before(rev)
    eye = np.eye(CHUNK, dtype=bool)
    same16 = (i[:, None] // 16) == (i[None, :] // 16)
    same32 = (i[:, None] // 32) == (i[None, :] // 32)
    masks = np.stack([before | eye, before, before & same16, before & same32 & ~same16,
                      before & ~same32, eye]).astype(np.float32)
    return jnp.asarray(masks)


def _unit_tri_inverse(a_all, m_blk, m_mid, m_top, eye):
    both = lambda f, xs, ys: [f(x, y) for x, y in zip(xs, ys)]
    a_blk = [a * m_blk for a in a_all]
    x = [eye + a for a in a_blk]
    pw = both(_dot_f32, a_blk, a_blk)
    for _ in range(2):
        x = both(lambda x_, p_: x_ + _dot_f32(x_, p_), x, pw)
        pw = both(_dot_f32, pw, pw)
    x = both(lambda x_, p_: x_ + _dot_f32(x_, p_), x, pw)
    for m in (m_mid, m_top):
        y = both(lambda x_, a_: _dot_f32(x_, a_ * m), x, a_all)
        x = both(lambda x_, y_: x_ + _dot_f32(y_, x_), x, y)
    return x


def _gdn_kernel(q_ref, k_ref, v_ref, gd_ref, grow_ref, m_ref, *rest, rev, finalize):
    if finalize:
        oprev_ref, gate_ref, nw_ref, o_ref, s_ref = rest
    else:
        o_ref, s_ref = rest

    @pl.when(pl.program_id(1) == 0)
    def _():
        s_ref[...] = jnp.zeros_like(s_ref)

    m_incl, m_strict, m_blk, m_mid, m_top, eye = (m_ref[n] for n in range(6))
    units = [(c, h) for c in _chunk_order(rev) for h in range(N_HEAD)]
    rows = lambda c: slice(c * CHUNK, (c + 1) * CHUNK)
    cols = lambda h: slice(h * GDN_DK, (h + 1) * GDN_DK)

    q = [q_ref[rows(c), cols(h)] for c, h in units]
    k = [k_ref[rows(c), cols(h)] for c, h in units]
    v = [v_ref[rows(c), cols(h)] for c, h in units]
    g_col = [gd_ref[rows(c), h:h + 1] for c, h in units]
    beta = [gd_ref[rows(c), N_HEAD + h:N_HEAD + h + 1] for c, h in units]
    g_row = [grow_ref[c, h:h + 1, :] for c, h in units]
    g_tot = [g[0:1] if rev else g[CHUNK - 1:CHUNK] for g in g_col]
    decay = [jnp.exp(jnp.where(m_incl > 0, gc - gr, -jnp.inf)) for gc, gr in zip(g_col, g_row)]
    k16 = [t.astype(BF16) for t in k]
    k_beta = [t * b for t, b in zip(k, beta)]
    a = [-(_dot_nt(kb.astype(BF16), kk) * d) * m_strict for kb, kk, d in zip(k_beta, k16, decay)]
    t16 = [(x - eye).astype(BF16) for x in _unit_tri_inverse(a, m_blk, m_mid, m_top, eye)]
    v_beta = [t * b for t, b in zip(v, beta)]
    e_col = [jnp.exp(g) for g in g_col]
    kbg = [kb * e for kb, e in zip(k_beta, e_col)]
    u = [vb + _dot(t, vb.astype(BF16)) for t, vb in zip(t16, v_beta)]
    w16 = [(x + _dot(t, x.astype(BF16))).astype(BF16) for t, x in zip(t16, kbg)]
    sc16 = [(_dot_nt(qq.astype(BF16), kk) * d).astype(BF16) for qq, kk, d in zip(q, k16, decay)]
    q_in = [(qq * e).astype(BF16) for qq, e in zip(q, e_col)]
    k_out = [(kk * jnp.exp(gt - gc)).astype(BF16) for kk, gt, gc in zip(k, g_tot, g_col)]
    a_end = [jnp.exp(gt) for gt in g_tot]

    s = [s_ref[h] for h in range(N_HEAD)]
    for ci in range(CPB):
        us = range(ci * N_HEAD, (ci + 1) * N_HEAD)
        s16 = [t.astype(BF16) for t in s]
        vn16 = [(u[n] - _dot(w16[n], s16[h])).astype(BF16) for h, n in enumerate(us)]
        o = [_dot(q_in[n], s16[h]) + _dot(sc16[n], vn16[h]) for h, n in enumerate(us)]
        s = [a_end[n] * s[h] + _dot_tn(k_out[n], vn16[h]) for h, n in enumerate(us)]
        for h, n in enumerate(us):
            c = units[n][0]
            oh = o[h]
            if finalize:
                oh = _finalize(oh, oprev_ref, gate_ref, nw_ref, c * CHUNK, h * GDN_DV, GDN_DV)
            o_ref[rows(c), cols(h)] = oh.astype(o_ref.dtype)
    for h in range(N_HEAD):
        s_ref[h] = s[h]


def _conv3(x, w):
    xp = jnp.pad(x, ((0, 0), (1, 1), (0, 0)))
    return xp[:, :-2] * w[0] + xp[:, 1:-1] * w[1] + xp[:, 2:] * w[2]


def _per_seq(fn, t, n_seq):
    C = t.shape[-1]
    n_lat_rows = t.shape[0] - n_seq * CTX_LEN
    yl = fn(t[:n_lat_rows].reshape(n_seq, -1, C))
    yc = fn(t[n_lat_rows:].reshape(n_seq, -1, C))
    return jnp.concatenate([yl.reshape(-1, yl.shape[-1]), yc.reshape(-1, yc.shape[-1])], axis=0)


def _gdn(p, p_small, conv_w, a_log, dt_bias, norm_w, n_seq, n_lat_blk):
    T = p.shape[0]

    def prep(z):
        z = _silu(_conv3(z, conv_w))
        q, k, v = jnp.split(z, 3, axis=-1)
        hn = lambda t: t.reshape(*t.shape[:-1], N_HEAD, GDN_DK)
        l2 = lambda t: (t * lax.rsqrt(jnp.sum(t * t, axis=-1, keepdims=True) + EPS)).reshape(*z.shape[:-1], -1)
        return jnp.concatenate([l2(hn(q)) * GDN_DK ** -0.5, l2(hn(k)), v], axis=-1)

    qkv = _per_seq(prep, p[:, CB_GDN_Q * COLW:(CB_GDN_Q + 3) * COLW], n_seq)
    a = p_small[:, SM_A:SM_A + 2 * N_HEAD].reshape(T, 2, N_HEAD)
    b = p_small[:, SM_B:SM_B + 2 * N_HEAD].reshape(T, 2, N_HEAD)
    g = (-jnp.exp(a_log) * jax.nn.softplus(a + dt_bias)).reshape(T // CHUNK, CHUNK, 2, N_HEAD)
    beta = jax.nn.sigmoid(b)
    out = None
    for rev in (True, False):
        finalize = not rev
        d = 1 if rev else 0
        gd_ = g[:, :, d]
        G = jnp.flip(jnp.cumsum(jnp.flip(gd_, 1), axis=1), 1) if rev else jnp.cumsum(gd_, axis=1)
        gd = jnp.concatenate([G.reshape(T, N_HEAD), beta[:, d]], axis=-1)
        grow = jnp.transpose(G, (0, 2, 1))
        rows = _scan_rows(n_seq, n_lat_blk, rev)
        extra = [(gd, pl.BlockSpec((SCAN_BLK, 2 * N_HEAD), lambda b_, j, rows=rows: (rows(b_, j), 0))),
                 (grow, pl.BlockSpec((CPB, N_HEAD, CHUNK), lambda b_, j, rows=rows: (rows(b_, j), 0, 0)))]
        kern = functools.partial(_gdn_kernel, rev=rev, finalize=finalize)
        out = _scan_call(kern, "gdn_bwd" if rev else "gdn_fwd", n_seq, n_lat_blk, rev, finalize,
                         [(qkv, 0), (qkv, 1), (qkv, 2)], [], [_gdn_consts(rev)], extra,
                         out, (p, CB_GDN_GATE), norm_w,
                         [pltpu.VMEM((N_HEAD, GDN_DK, GDN_DV), F32)])
    return out


GLA_SUB = 4


def _gla_consts(rev):
    C = CHUNK
    idx = np.arange(C)
    before = _before(rev)
    incl = before | np.eye(C, dtype=bool)
    sizes = (16, 4, 1)
    blk = [idx // s for s in sizes]
    par = [idx // 64, idx // 16, idx // 4]
    sub = [b % GLA_SUB for b in blk]

    def blk_before(l, a, b):
        return (blk[l][a] > blk[l][b]) if rev else (blk[l][a] < blk[l][b])

    pq = np.zeros((2, C, C), np.float32)
    for l in range(2):
        pq[l] = incl & (blk[l][:, None] == blk[l][None, :])
    kmt = np.zeros((3, C, GLA_SUB * C), np.float32)
    maskx = np.zeros((4, C, GLA_SUB * C), np.float32)
    for l in range(3):
        for m in range(GLA_SUB):
            for j in range(C):
                tgt = par[l][j] * GLA_SUB + m
                valid = (tgt < blk[l][j]) if rev else (tgt > blk[l][j])
                if not valid:
                    continue
                t_after_j = before[:, j]
                t_blk_before_tgt = (blk[l] > tgt) if rev else (blk[l] < tgt)
                kmt[l, :, m * C + j] = t_after_j & t_blk_before_tgt
            for i in range(C):
                if sub[l][i] != m:
                    continue
                sel = (par[l] == par[l][i]) & np.array([blk_before(l, j, i) for j in range(C)])
                maskx[l, i, m * C:(m + 1) * C] = sel
    maskx[3, :, :C] = np.eye(C)
    id4 = np.tile(np.eye(C, dtype=np.float32), (1, GLA_SUB))
    return (jnp.asarray(incl.astype(np.float32), BF16), jnp.asarray(pq, BF16), jnp.asarray(kmt, BF16),
            jnp.asarray(maskx, F32), jnp.asarray(id4, BF16))


def _gla_kernel(qk_ref, v_ref, sm_ref, w2_ref, b2_ref, incl_ref, pq_ref, kmt_ref, mx_ref, id4_ref, *rest,
                rev, finalize):
    if finalize:
        oprev_ref, gate_ref, nw_ref, o_ref, st_ref = rest
    else:
        o_ref, st_ref = rest

    @pl.when(pl.program_id(1) == 0)
    def _():
        st_ref[...] = jnp.zeros_like(st_ref)

    hw = N_HEAD * GLA_DK
    x = _dot(sm_ref[...].astype(BF16), w2_ref[...]) + b2_ref[...]
    gk_all = (jnp.minimum(x, 0.0) - jnp.log(1.0 + jnp.exp(-jnp.abs(x)))) * (1.0 / GLA_TAU)
    incl = incl_ref[...]
    id4 = id4_ref[...]
    order = list(_chunk_order(rev))
    units = [(ci, h) for ci in range(CPB) for h in range(N_HEAD)]
    rows = lambda c: slice(c * CHUNK, (c + 1) * CHUNK)
    kcols = lambda h: slice(h * GLA_DK, (h + 1) * GLA_DK)
    vcols = lambda h: slice(h * GLA_DV, (h + 1) * GLA_DV)

    gk = [gk_all[rows(c)] for c in order]
    q = [qk_ref[rows(c), :hw] * GLA_DK ** -0.5 for c in order]
    k = [qk_ref[rows(c), hw:] for c in order]
    vb = [v_ref[rows(c), :].astype(BF16) for c in order]
    G = [_dot_const(incl, g, 3) for g in gk]
    g_tot = [g[0:1] if rev else g[CHUNK - 1:CHUNK] for g in G]
    e1 = [_dot_const(pq_ref[0], g, 3) for g in gk]
    e2 = [_dot_const(pq_ref[1], g, 3) for g in gk]
    q_lvl = [[(qq * jnp.exp(e)).astype(BF16) for qq, e in zip(q, es)] for es in (e1, e2, gk)]
    q_lvl.append([qq.astype(BF16) for qq in q])
    q_in = [(qq * jnp.exp(g)).astype(BF16) for qq, g in zip(q, G)]
    k_out = [(kk * jnp.exp(gt - g)).astype(BF16) for kk, gt, g in zip(k, g_tot, G)]
    kb = [kk.astype(BF16) for kk in k]
    decay_tot = [jnp.exp(gt) for gt in g_tot]

    kt = [_dot_tn(kb[ci][:, kcols(h)], id4) for ci, h in units]
    sx = [_dot(q_lvl[3][ci][:, kcols(h)], t.astype(BF16)) * mx_ref[3] for (ci, h), t in zip(units, kt)]
    for l in range(3):
        e = [_dot_tn_const(gk[ci][:, kcols(h)], kmt_ref[l], 2) for ci, h in units]
        kt_l = [(t * jnp.exp(e_)).astype(BF16) for t, e_ in zip(kt, e)]
        sx = [s_ + _dot(q_lvl[l][ci][:, kcols(h)], t) * mx_ref[l] for (ci, h), s_, t in zip(units, sx, kt_l)]
    vh = [vb[ci][:, vcols(h)] for ci, h in units]
    o_intra = [_dot(s_.astype(BF16), jnp.concatenate([v_] * GLA_SUB, axis=0)) for s_, v_ in zip(sx, vh)]
    kv = [_dot_tn(v_, k_out[ci][:, kcols(h)]) for (ci, h), v_ in zip(units, vh)]

    st = [st_ref[h] for h in range(N_HEAD)]
    for ci in range(CPB):
        c = order[ci]
        for h in range(N_HEAD):
            n = ci * N_HEAD + h
            o = o_intra[n] + _dot_nt(q_in[ci][:, kcols(h)], st[h].astype(BF16))
            st[h] = st[h] * decay_tot[ci][:, kcols(h)] + kv[n]
            if finalize:
                o = _finalize(o, oprev_ref, gate_ref, nw_ref, c * CHUNK, h * GLA_DV, GLA_DV)
            o_ref[rows(c), vcols(h)] = o.astype(o_ref.dtype)
    for h in range(N_HEAD):
        st_ref[h] = st[h]


def _gla(p, p_small, w2, b2, norm_w, n_seq, n_lat_blk):
    out = None
    hw = N_HEAD * GLA_DK
    for rev in (True, False):
        finalize = not rev
        d = 1 if rev else 0
        lr0 = SM_LR + d * GLA_RANK
        w2p = jnp.zeros((V7X_LANES, hw), F32).at[lr0:lr0 + GLA_RANK].set(w2[d]).astype(BF16)
        rows = _scan_rows(n_seq, n_lat_blk, rev)
        extra = [(p_small, pl.BlockSpec((SCAN_BLK, V7X_LANES), lambda b_, j, rows=rows: (rows(b_, j), 0)))]
        kern = functools.partial(_gla_kernel, rev=rev, finalize=finalize)
        out = _scan_call(kern, "gla_bwd" if rev else "gla_fwd", n_seq, n_lat_blk, rev, finalize,
                         [(p, CB_GLA_QK), (p, CB_GLA_V)], [], [w2p, b2[d].reshape(1, hw), *_gla_consts(rev)],
                         extra, out, (p, CB_GLA_GATE), norm_w,
                         [pltpu.VMEM((N_HEAD, GLA_DV, GLA_DK), F32)])
    return out


def _hyena_filters(L, w1, b1, w2, b2, w3, freq):
    t = jnp.linspace(0.0, 1.0, L, dtype=F32)[:, None]
    w = 2.0 * math.pi * jnp.arange(L, dtype=F32) / L
    f = jnp.linspace(1e-4, HY_BANDS - 1, HY_BANDS, dtype=F32)
    ang = w[:, None] * f[None, :]
    z = jnp.concatenate([t, jnp.cos(ang), -jnp.sin(ang)], axis=-1)
    hdn = jnp.sin(freq[0] * (z @ w1 + b1))
    for i in range(HY_INNER):
        hdn = jnp.sin(freq[i + 1] * (hdn @ w2[i] + b2[i]))
    h = (hdn @ w3).reshape(L, HY_ORDER, 2, HY_W)
    deltas = jnp.abs(jnp.linspace(HY_MIN_DECAY, HY_MAX_DECAY, HY_W, dtype=F32))
    h = h * jnp.exp(-t[:, :, None, None] * deltas)
    hf, hb = h[:, :, 0], h[:, :, 1]
    hb = hb.at[0].set(0.0)
    scale = lax.rsqrt(jnp.sum(hf * hf, axis=0) + jnp.sum(hb * hb, axis=0) + EPS)
    return (hf * scale).reshape(L, -1), (hb * scale).reshape(L, -1)


def _dft_mats(L):
    k = jnp.arange(L, dtype=jnp.int32)[:, None]
    n = jnp.arange(L, dtype=jnp.int32)[None, :]
    ang = (((2 * k + 1) * n) % (4 * L)).astype(F32) * (math.pi / (2 * L))
    return tuple(_split_bf16(jnp.cos(ang), 2)) + tuple(_split_bf16(jnp.sin(ang), 2))


def _hy_spectra(hf, hb, dft):
    c_hi, c_lo, s_hi, s_lo = dft

    def mm3(m_hi, m_lo, g):
        g_hi, g_lo = _split_bf16(g, 2)
        return _matmul(m_hi, g_hi) + (_matmul(m_hi, g_lo) + _matmul(m_lo, g_hi))

    return mm3(c_hi, c_lo, hf + hb), mm3(s_hi, s_lo, hb - hf)


def _hy_kernel(z_ref, x_ref, skip_ref, c_ref, s_ref, kr_ref, ki_ref, *rest, last, n_k, slab, inv_scale):
    if last:
        gate_ref, o_ref, zb_ref, acc_ref = rest
    else:
        o_ref, zb_ref, acc_ref = rest
    kt = pl.program_id(2)

    @pl.when(kt == 0)
    def _():
        zb_ref[...] = z_ref[...].astype(BF16)
        acc_ref[...] = jnp.zeros_like(acc_ref)

    c, s = c_ref[...], s_ref[...]
    zb = zb_ref[...]
    zr, zi = _dot(c, zb), -_dot(s, zb)
    kr, ki = kr_ref[...], ki_ref[...]
    pr = (zr * kr - zi * ki).T.astype(BF16)
    pi = (zr * ki + zi * kr).T.astype(BF16)
    acc_ref[...] += _dot(pr, c) - _dot(pi, s)

    @pl.when(kt == n_k - 1)
    def _():
        for r0 in range(0, acc_ref.shape[1], slab):
            conv = acc_ref[:, r0:r0 + slab].T * inv_scale
            z = z_ref[r0:r0 + slab, :]
            y = x_ref[r0:r0 + slab, :] * (conv + skip_ref[...] * z)
            if last:
                y = y * _silu(gate_ref[r0:r0 + slab, :])
            o_ref[r0:r0 + slab, :] = y.astype(o_ref.dtype)


HY_WT = 256


def _hy_order(z_src, x_src, gate_src, skip, dft, kr, ki, order, n_seq, L):
    last = gate_src is not None
    tk = min(512, L)
    n_k = L // tk
    slab = min(512, L)
    n_wt = HY_W // HY_WT

    def rowspec(src):
        _, r0, c0 = src
        return pl.BlockSpec((L, HY_WT), lambda b, w, k: (r0 + b, c0 + w), pipeline_mode=pl.Buffered(1))

    srcs = [z_src, x_src] + ([gate_src] if last else [])
    twid = pl.BlockSpec((tk, L), lambda b, w, k: (k, 0))
    spec = pl.BlockSpec((tk, HY_WT), lambda b, w, k: (k, order * n_wt + w))
    kern = functools.partial(_hy_kernel, last=last, n_k=n_k, slab=slab, inv_scale=1.0 / L)
    args = [z_src[0], x_src[0], skip[order].reshape(1, HY_W), dft[0], dft[2], kr, ki]
    specs = [rowspec(z_src), rowspec(x_src), pl.BlockSpec((1, HY_WT), lambda b, w, k: (0, w)),
             twid, twid, spec, spec]
    if last:
        args.append(gate_src[0])
        specs.append(rowspec(gate_src))
    return pl.pallas_call(
        kern,
        grid=(n_seq, n_wt, n_k),
        in_specs=specs,
        out_specs=pl.BlockSpec((L, HY_WT), lambda b, w, k: (b, w)),
        out_shape=jax.ShapeDtypeStruct((n_seq * L, HY_W), BF16 if last else F32),
        scratch_shapes=[pltpu.VMEM((L, HY_WT), BF16), pltpu.VMEM((HY_WT, L), F32)],
        compiler_params=_cparams(("parallel", "parallel", "arbitrary")),
        name="hyena_order%d_L%d" % (order, L),
    )(*args)


def _hyena(p, conv_w, conv_b, w1, b1, w2, b2, w3, freq, skip, dfts, n_seq, seq_len):
    zc = _per_seq(lambda z: _conv3(z, conv_w) + conv_b, p[:, :3 * HY_W], n_seq)
    n_wt = HY_W // HY_WT
    outs = []
    for L, row0 in ((seq_len, 0), (CTX_LEN, n_seq * seq_len // CTX_LEN)):
        dft = dfts[L]
        kr, ki = _hy_spectra(*_hyena_filters(L, w1, b1, w2, b2, w3, freq), dft)
        y1 = _hy_order((zc, row0, 0), (zc, row0, n_wt), None, skip, dft, kr, ki, 0, n_seq, L)
        outs.append(_hy_order((y1, 0, 0), (zc, row0, 2 * n_wt), (p, row0, CB_HY_GATE * n_wt),
                              skip, dft, kr, ki, 1, n_seq, L))
    return jnp.concatenate(outs, axis=0)


def _layout_w_in(w_in):
    sizes = [3 * HY_W, HY_W, 3 * BR_W, 2 * N_HEAD, 2 * N_HEAD, BR_W, 2 * N_HEAD * RET_DK, BR_W, BR_W,
             2 * N_HEAD * GLA_DK, BR_W, 2 * GLA_RANK, BR_W]
    offs = np.concatenate([[0], np.cumsum(sizes)])
    (hy_proj, hy_gate, gdn_qkv, gdn_a, gdn_b, gdn_gate, ret_qk, ret_v, ret_gate,
     gla_qk, gla_v, gla_lr, gla_gate) = [w_in[:, offs[n]:offs[n + 1]] for n in range(len(sizes))]
    D = w_in.shape[0]
    half = RET_DK // 2
    ret_rot = ret_qk.reshape(D, 2 * N_HEAD, 2, half)[:, :, ::-1].reshape(D, -1)
    main = jnp.concatenate([hy_proj, hy_gate, gdn_qkv, gdn_gate, ret_qk, ret_rot, ret_v, ret_gate,
                            gla_qk, gla_v, gla_gate], axis=1)
    small = jnp.concatenate([gdn_a, gdn_b, gla_lr], axis=1)
    small = jnp.pad(small, ((0, 0), (0, V7X_LANES - small.shape[1])))
    return main, small


def _layer(x, mod, dims, rope, dfts, norm_w, w_in, hy_conv_w, hy_conv_b, hy_w1, hy_b1, hy_w2, hy_b2, hy_w3,
           hy_freq, hy_skip, gdn_conv_w, gdn_a_log, gdn_dt_bias, gdn_norm, ret_norm, gla_w2, gla_b2,
           gla_norm, w_branch, w_merge, b_merge, w_out):
    n_seq, seq_len = dims
    n_lat_rows = n_seq * seq_len
    n_lat_blk = seq_len // SCAN_BLK
    shift, scale, gate = mod
    w_main, w_small = _layout_w_in(w_in.astype(BF16))
    p, h = _inproj(x, norm_w, scale, shift, w_main, n_lat_rows, seq_len)
    p_small = _matmul(h, w_small)

    y_hy = _hyena(p, hy_conv_w, hy_conv_b, hy_w1, hy_b1, hy_w2, hy_b2, hy_w3, hy_freq, hy_skip, dfts,
                  n_seq, seq_len)
    y_gdn = _gdn(p, p_small, gdn_conv_w, gdn_a_log, gdn_dt_bias, gdn_norm, n_seq, n_lat_blk)
    y_ret = _retention(p, rope[0], rope[1], ret_norm, n_seq, n_lat_blk)
    y_gla = _gla(p, p_small, gla_w2, gla_b2, gla_norm, n_seq, n_lat_blk)

    merged = _merge(h, (y_hy, y_gdn, y_ret, y_gla), w_merge.astype(BF16), b_merge, w_branch.astype(BF16))
    return _outproj(merged, w_out.astype(BF16), x, gate, n_lat_rows, seq_len)


def kernel(x, c, ctx, c_ctx, norm_w, ada_w, ada_b, w_in, hy_conv_w, hy_conv_b, hy_w1, hy_b1,
           hy_w2, hy_b2, hy_w3, hy_freq, hy_skip, gdn_conv_w, gdn_a_log, gdn_dt_bias,
           gdn_norm, ret_norm, gla_w2, gla_b2, gla_norm, w_branch, w_merge, b_merge, w_out,
           final_norm):
    B, S, D = x.shape
    layer_params = (norm_w, w_in, hy_conv_w, hy_conv_b, hy_w1, hy_b1, hy_w2, hy_b2, hy_w3, hy_freq,
                    hy_skip, gdn_conv_w, gdn_a_log, gdn_dt_bias, gdn_norm, ret_norm, gla_w2, gla_b2,
                    gla_norm, w_branch, w_merge, b_merge, w_out)
    xt = jnp.concatenate([x.reshape(-1, D), ctx.reshape(-1, D)], axis=0)
    cond = _silu(jnp.concatenate([c_ctx[None], c], axis=0))
    cond = jnp.pad(cond, ((0, -(1 + B) % 8), (0, 0))).astype(BF16)
    rope = _rope_tables(S)
    dfts = {L: _dft_mats(L) for L in (S, CTX_LEN)}
    for l in range(DEPTH):
        m = (_matmul(cond, ada_w[l].astype(BF16)) + ada_b[l])[:1 + B]
        mod = tuple(t.reshape(1 + B, 1, D) for t in jnp.split(m, 3, axis=-1))
        xt = _layer(xt, mod, (B, S), rope, dfts, *[p[l] for p in layer_params])
    return _final_norm(xt, final_norm, B * S).reshape(B, S, D)
```

```python
import functools
import math

import numpy as np
import jax
import jax.numpy as jnp
from jax import lax
from jax.experimental import pallas as pl
from jax.experimental.pallas import tpu as pltpu

D_MODEL = 2048
DEPTH = 4
CTX_LEN = 256
GRID_W = 64
F32 = jnp.float32
BF16 = jnp.bfloat16
EPS = 1e-6
CHUNK = 64

N_BRANCH = 4
BR_W = D_MODEL // N_BRANCH

HY_W = BR_W
HY_ORDER = 2
HY_EMB = 33
HY_BANDS = (HY_EMB - 1) // 2
HY_INNER = 2
HY_FAST_DECAY = 0.3
HY_SLOW_DECAY = 1.5
HY_TARGET = 1e-2
HY_MIN_DECAY = math.log(HY_TARGET) / HY_SLOW_DECAY
HY_MAX_DECAY = math.log(HY_TARGET) / HY_FAST_DECAY

N_HEAD = 4
GDN_DK = BR_W // N_HEAD
GDN_DV = BR_W // N_HEAD
RET_DK = BR_W // (2 * N_HEAD)
RET_DV = BR_W // N_HEAD
RET_ROPE_PAIRS = (8, 12, 12)
ROPE_BASE = 10000.0
GLA_DK = BR_W // (2 * N_HEAD)
GLA_DV = BR_W // N_HEAD
GLA_RANK = 16
GLA_TAU = 16.0

V7X_LANES = 128
VMEM_LIMIT_BYTES = 56 * 1024 * 1024

SCAN_BLK = CTX_LEN
CPB = SCAN_BLK // CHUNK
COLW = BR_W

CB_HY_V, CB_HY_X1, CB_HY_X2, CB_HY_GATE = 0, 1, 2, 3
CB_GDN_Q, CB_GDN_GATE = 4, 7
CB_RET_QK, CB_RET_QKROT, CB_RET_V, CB_RET_GATE = 8, 9, 10, 11
CB_GLA_QK, CB_GLA_V, CB_GLA_GATE = 12, 13, 14
N_COLBLK = 15
SM_A, SM_B, SM_LR = 0, 2 * N_HEAD, 4 * N_HEAD


def _dot(a, b):
    return jnp.dot(a, b, preferred_element_type=F32)


def _dot_nt(a, b):
    return lax.dot_general(a, b, (((1,), (1,)), ((), ())), preferred_element_type=F32)


def _dot_tn(a, b):
    return lax.dot_general(a, b, (((0,), (0,)), ((), ())), preferred_element_type=F32)


def _split_bf16(x, terms):
    parts = []
    r = x
    for t in range(terms):
        p = r.astype(BF16)
        parts.append(p)
        if t + 1 < terms:
            r = r - p.astype(F32)
    return parts


def _dot_f32(a, b):
    a1, a2 = _split_bf16(a, 2)
    b1, b2 = _split_bf16(b, 2)
    return _dot(a1, b1) + (_dot(a1, b2) + _dot(a2, b1))


def _dot_const(c, x, terms):
    out = None
    for p in _split_bf16(x, terms):
        t = _dot(c, p)
        out = t if out is None else out + t
    return out


def _dot_tn_const(x, c, terms):
    out = None
    for p in _split_bf16(x, terms):
        t = _dot_tn(p, c)
        out = t if out is None else out + t
    return out


def _silu(x):
    return x * jax.nn.sigmoid(x)


def _cparams(sem):
    return pltpu.CompilerParams(dimension_semantics=sem, vmem_limit_bytes=VMEM_LIMIT_BYTES)


def _mm_kernel(a_ref, b_ref, o_ref):
    o_ref[...] = _dot(a_ref[...], b_ref[...]).astype(o_ref.dtype)


def _pick_tile(n, cands):
    for c in cands:
        if n % c == 0:
            return c
    return n


def _matmul(a, b, out_dtype=F32):
    M, K = a.shape
    _, N = b.shape
    tm = _pick_tile(M, (1024, 512, 256, 128, 64, 32, 16, 8))
    tn = _pick_tile(N, (512, 384, 256, 128))
    return pl.pallas_call(
        _mm_kernel,
        grid=(M // tm, N // tn),
        in_specs=[pl.BlockSpec((tm, K), lambda i, j: (i, 0)),
                  pl.BlockSpec((K, tn), lambda i, j: (0, j))],
        out_specs=pl.BlockSpec((tm, tn), lambda i, j: (i, j)),
        out_shape=jax.ShapeDtypeStruct((M, N), out_dtype),
        compiler_params=_cparams(("parallel", "arbitrary")),
        name="matmul",
    )(a, b)


def _mod_row(n_lat_blk, blk_per_seq):
    return lambda i: jnp.where(i < n_lat_blk, 1 + i // blk_per_seq, 0)


def _inproj_kernel(x_ref, nw_ref, sc_ref, sh_ref, w_ref, p_ref, h_ref, hs_ref):
    @pl.when(pl.program_id(1) == 0)
    def _():
        for r0 in range(0, x_ref.shape[0], PREP_SLAB):
            x = x_ref[r0:r0 + PREP_SLAB, :]
            y = x * lax.rsqrt(jnp.mean(x * x, axis=-1, keepdims=True) + EPS) * nw_ref[...]
            h = (y * (1.0 + sc_ref[0]) + sh_ref[0]).astype(BF16)
            hs_ref[r0:r0 + PREP_SLAB, :] = h
            h_ref[r0:r0 + PREP_SLAB, :] = h

    p_ref[...] = _dot(hs_ref[...], w_ref[...]).astype(p_ref.dtype)


def _inproj(x, norm_w, scale, shift, w, n_lat_rows, seq_len):
    T, D = x.shape
    N = w.shape[1]
    tm, tn = 1024, COLW
    row = _mod_row(n_lat_rows // tm, seq_len // tm)
    return pl.pallas_call(
        _inproj_kernel,
        grid=(T // tm, N // tn),
        in_specs=[pl.BlockSpec((tm, D), lambda i, j: (i, 0)),
                  pl.BlockSpec((1, D), lambda i, j: (0, 0)),
                  pl.BlockSpec((1, 1, D), lambda i, j: (row(i), 0, 0)),
                  pl.BlockSpec((1, 1, D), lambda i, j: (row(i), 0, 0)),
                  pl.BlockSpec((D, tn), lambda i, j: (0, j))],
        out_specs=[pl.BlockSpec((tm, tn), lambda i, j: (i, j)),
                   pl.BlockSpec((tm, D), lambda i, j: (i, 0))],
        out_shape=[jax.ShapeDtypeStruct((T, N), BF16), jax.ShapeDtypeStruct((T, D), BF16)],
        scratch_shapes=[pltpu.VMEM((tm, D), BF16)],
        compiler_params=_cparams(("parallel", "arbitrary")),
        name="inproj",
    )(x, norm_w.reshape(1, D), scale, shift, w)


def _merge_kernel(h_ref, y0_ref, y1_ref, y2_ref, y3_ref, wm_ref, bm_ref, wb_ref, o_ref):
    h = h_ref[...]
    acc = None
    for i, y_ref in enumerate((y0_ref, y1_ref, y2_ref, y3_ref)):
        g = jax.nn.sigmoid(_dot(h, wm_ref[i]) + bm_ref[i])
        t = g * _dot(y_ref[...], wb_ref[i])
        acc = t if acc is None else acc + t
    o_ref[...] = acc.astype(o_ref.dtype)


def _merge(h, ys, w_merge, b_merge, w_branch):
    T, D = h.shape
    tm, tn = 1024, 256
    yspec = pl.BlockSpec((tm, BR_W), lambda i, j: (i, 0))
    return pl.pallas_call(
        _merge_kernel,
        grid=(T // tm, D // tn),
        in_specs=[pl.BlockSpec((tm, D), lambda i, j: (i, 0)), yspec, yspec, yspec, yspec,
                  pl.BlockSpec((N_BRANCH, D, tn), lambda i, j: (0, 0, j)),
                  pl.BlockSpec((N_BRANCH, 1, tn), lambda i, j: (0, 0, j)),
                  pl.BlockSpec((N_BRANCH, BR_W, tn), lambda i, j: (0, 0, j))],
        out_specs=pl.BlockSpec((tm, tn), lambda i, j: (i, j)),
        out_shape=jax.ShapeDtypeStruct((T, D), BF16),
        compiler_params=_cparams(("parallel", "arbitrary")),
        name="merge",
    )(h, *ys, w_merge, b_merge.reshape(N_BRANCH, 1, D), w_branch)


def _outproj_kernel(a_ref, w_ref, x_ref, g_ref, o_ref):
    o_ref[...] = x_ref[...] + g_ref[0] * _dot(a_ref[...], w_ref[...])


def _outproj(a, w, x, gate, n_lat_rows, seq_len):
    T, D = x.shape
    tm, tn = 1024, 512
    row = _mod_row(n_lat_rows // tm, seq_len // tm)
    return pl.pallas_call(
        _outproj_kernel,
        grid=(T // tm, D // tn),
        in_specs=[pl.BlockSpec((tm, D), lambda i, j: (i, 0)),
                  pl.BlockSpec((D, tn), lambda i, j: (0, j)),
                  pl.BlockSpec((tm, tn), lambda i, j: (i, j)),
                  pl.BlockSpec((1, 1, tn), lambda i, j: (row(i), 0, j))],
        out_specs=pl.BlockSpec((tm, tn), lambda i, j: (i, j)),
        out_shape=jax.ShapeDtypeStruct((T, D), F32),
        compiler_params=_cparams(("parallel", "arbitrary")),
        name="outproj",
    )(a, w, x, gate)


def _final_norm_kernel(x_ref, w_ref, o_ref):
    x = x_ref[...]
    o_ref[...] = x * lax.rsqrt(jnp.mean(x * x, axis=-1, keepdims=True) + EPS) * w_ref[...]


def _final_norm(x, w, n_lat_rows):
    D = x.shape[1]
    tm = 512
    return pl.pallas_call(
        _final_norm_kernel,
        grid=(n_lat_rows // tm,),
        in_specs=[pl.BlockSpec((tm, D), lambda i: (i, 0)),
                  pl.BlockSpec((1, D), lambda i: (0, 0))],
        out_specs=pl.BlockSpec((tm, D), lambda i: (i, 0)),
        out_shape=jax.ShapeDtypeStruct((n_lat_rows, D), F32),
        compiler_params=_cparams(("parallel",)),
        name="final_norm",
    )(x, w.reshape(1, D))


def _scan_rows(n_seq, n_lat_blk, rev):
    def idx(b, j):
        jj = (n_lat_blk - j) if rev else (j - 1)
        return jnp.where(j == 0, n_seq * n_lat_blk + b, b * n_lat_blk + jj)
    return idx


def _chunk_order(rev):
    return range(CPB - 1, -1, -1) if rev else range(CPB)


def _before(rev):
    i = np.arange(CHUNK)[:, None]
    j = np.arange(CHUNK)[None, :]
    return (j > i) if rev else (j < i)


def _finalize(o, oprev_ref, gate_ref, nw_ref, r0, c0, width):
    o = o + oprev_ref[r0:r0 + CHUNK, c0:c0 + width]
    o = o * lax.rsqrt(jnp.mean(o * o, axis=-1, keepdims=True) + EPS) * nw_ref[...]
    return o * _silu(gate_ref[r0:r0 + CHUNK, c0:c0 + width].astype(F32))


def _scan_call(kernel, name, n_seq, n_lat_blk, rev, finalize, row_inputs, tab_inputs, const_inputs,
               extra_specs_inputs, oprev, gate_src, norm_w, scratch):
    rows = _scan_rows(n_seq, n_lat_blk, rev)
    T = row_inputs[0][0].shape[0]
    args, specs = [], []
    for arr, cb in row_inputs:
        args.append(arr)
        specs.append(pl.BlockSpec((SCAN_BLK, COLW), functools.partial(
            lambda b, j, cb: (rows(b, j), cb), cb=cb)))
    for arr in tab_inputs:
        args.append(arr)
        specs.append(pl.BlockSpec((SCAN_BLK, arr.shape[1]), lambda b, j: (
            jnp.where(j == 0, 0, 1 + ((n_lat_blk - j) if rev else (j - 1))), 0)))
    for arr, spec in extra_specs_inputs:
        args.append(arr)
        specs.append(spec)
    for arr in const_inputs:
        args.append(arr)
        specs.append(pl.BlockSpec(arr.shape, functools.partial(lambda b, j, n: (0,) * n, n=arr.ndim)))
    if finalize:
        args += [oprev, gate_src[0], norm_w.reshape(1, -1)]
        specs += [pl.BlockSpec((SCAN_BLK, COLW), lambda b, j: (rows(b, j), 0)),
                  pl.BlockSpec((SCAN_BLK, COLW), functools.partial(
                      lambda b, j, cb: (rows(b, j), cb), cb=gate_src[1])),
                  pl.BlockSpec((1, norm_w.shape[0]), lambda b, j: (0, 0))]
    return pl.pallas_call(
        kernel,
        grid=(n_seq, 1 + n_lat_blk),
        in_specs=specs,
        out_specs=pl.BlockSpec((SCAN_BLK, COLW), lambda b, j: (rows(b, j), 0)),
        out_shape=jax.ShapeDtypeStruct((T, COLW), BF16 if finalize else F32),
        scratch_shapes=scratch,
        compiler_params=_cparams(("parallel", "arbitrary")),
        name=name,
    )(*args)


def _ret_consts(rev):
    lg = np.log(1.0 - np.power(2.0, -5.0 - np.arange(N_HEAD, dtype=np.float64)))
    i = np.arange(CHUNK, dtype=np.float64)
    steps = (CHUNK - i) if rev else (i + 1.0)
    G = steps[None, :] * lg[:, None]
    mask = _before(rev) | np.eye(CHUNK, dtype=bool)
    dm = np.where(mask[None], np.exp(G[:, :, None] - G[:, None, :]), 0.0)
    qs = np.repeat(np.exp(G).T, RET_DK, axis=1)
    ks = np.repeat(np.exp(CHUNK * lg[:, None] - G).T, RET_DK, axis=1)
    a_end = tuple(float(v) for v in np.exp(CHUNK * lg))
    return (jnp.asarray(dm, F32), jnp.asarray(qs, F32), jnp.asarray(ks, F32)), a_end


def _ret_kernel(qk_ref, qkr_ref, v_ref, cos_ref, sin_ref, dm_ref, qs_ref, ks_ref, *rest,
                rev, finalize, a_end):
    if finalize:
        oprev_ref, gate_ref, nw_ref, o_ref, s_ref = rest
    else:
        o_ref, s_ref = rest

    @pl.when(pl.program_id(1) == 0)
    def _():
        s_ref[...] = jnp.zeros_like(s_ref)

    hw = N_HEAD * RET_DK
    qk = (qk_ref[...].astype(F32) * cos_ref[...]
          + qkr_ref[...].astype(F32) * sin_ref[...])
    order = list(_chunk_order(rev))
    units = [(ci, h) for ci in range(CPB) for h in range(N_HEAD)]
    rows = lambda c: slice(c * CHUNK, (c + 1) * CHUNK)
    kcols = lambda h: slice(h * RET_DK, (h + 1) * RET_DK)
    vcols = lambda h: slice(h * RET_DV, (h + 1) * RET_DV)

    qc = [qk[rows(c), :hw] for c in order]
    kc = [qk[rows(c), hw:] for c in order]
    qb = [t.astype(BF16) for t in qc]
    kb = [t.astype(BF16) for t in kc]
    q_in = [(t * qs_ref[...]).astype(BF16) for t in qc]
    k_out = [(t * ks_ref[...]).astype(BF16) for t in kc]
    vb = [v_ref[rows(c), :].astype(BF16) for c in order]
    vh = [vb[ci][:, vcols(h)] for ci, h in units]
    sc = [(_dot_nt(qb[ci][:, kcols(h)], kb[ci][:, kcols(h)]) * dm_ref[h]).astype(BF16) for ci, h in units]
    o_intra = [_dot(s_, v_) for s_, v_ in zip(sc, vh)]
    kv = [_dot_tn(k_out[ci][:, kcols(h)], v_) for (ci, h), v_ in zip(units, vh)]

    s = [s_ref[h] for h in range(N_HEAD)]
    for ci in range(CPB):
        c = order[ci]
        for h in range(N_HEAD):
            n = ci * N_HEAD + h
            o = o_intra[n] + _dot(q_in[ci][:, kcols(h)], s[h].astype(BF16))
            s[h] = a_end[h] * s[h] + kv[n]
            if finalize:
                o = _finalize(o, oprev_ref, gate_ref, nw_ref, c * CHUNK, h * RET_DV, RET_DV)
            o_ref[rows(c), vcols(h)] = o.astype(o_ref.dtype)
    for h in range(N_HEAD):
        s_ref[h] = s[h]


def _retention(p, cos_tab, sin_tab, norm_w, n_seq, n_lat_blk):
    out = None
    for rev in (True, False):
        finalize = not rev
        consts, a_end = _ret_consts(rev)
        kern = functools.partial(_ret_kernel, rev=rev, finalize=finalize, a_end=a_end)
        out = _scan_call(kern, "retention_bwd" if rev else "retention_fwd", n_seq, n_lat_blk, rev, finalize,
                         [(p, CB_RET_QK), (p, CB_RET_QKROT), (p, CB_RET_V)], [cos_tab, sin_tab],
                         list(consts), [], out, (p, CB_RET_GATE), norm_w,
                         [pltpu.VMEM((N_HEAD, RET_DK, RET_DV), F32)])
    return out


def _rope_tables(seq_len):
    def angles(pos):
        angs = []
        for a, n in enumerate(RET_ROPE_PAIRS):
            freqs = ROPE_BASE ** (-jnp.arange(n, dtype=F32) / n)
            angs.append(pos[:, a:a + 1] * freqs)
        return jnp.concatenate(angs, axis=-1)

    t = jnp.arange(CTX_LEN, dtype=F32)
    zero = jnp.zeros((CTX_LEN,), F32)
    pos_c = jnp.stack([t, zero, zero], axis=-1)
    rows = seq_len // GRID_W
    r = jnp.repeat(jnp.arange(rows, dtype=F32), GRID_W)
    col = jnp.broadcast_to(jnp.arange(GRID_W, dtype=F32), (rows, GRID_W)).reshape(-1)
    pos_l = jnp.stack([jnp.full((seq_len,), CTX_LEN, F32), r, col], axis=-1)
    ang = jnp.concatenate([angles(pos_c), angles(pos_l)], axis=0)
    cos = jnp.concatenate([jnp.cos(ang), jnp.cos(ang)], axis=-1)
    sin = jnp.concatenate([-jnp.sin(ang), jnp.sin(ang)], axis=-1)
    scale = jnp.concatenate([jnp.ones((N_HEAD * RET_DK,), F32),
                             jnp.full((N_HEAD * RET_DK,), RET_DK ** -0.5, F32)])
    return jnp.tile(cos, (1, 2 * N_HEAD)) * scale, jnp.tile(sin, (1, 2 * N_HEAD)) * scale


def _gdn_consts(rev):
    i = np.arange(CHUNK)
    before = _before(rev)
    eye = np.eye(CHUNK, dtype=bool)
    same16 = (i[:, None] // 16) == (i[None, :] // 16)
    same32 = (i[:, None] // 32) == (i[None, :] // 32)
    masks = np.stack([before | eye, before, before & same16, before & same32 & ~same16,
                      before & ~same32, eye]).astype(np.float32)
    return jnp.asarray(masks)


def _unit_tri_inverse(a_all, m_blk, m_mid, m_top, eye):
    both = lambda f, xs, ys: [f(x, y) for x, y in zip(xs, ys)]
    a_blk = [a * m_blk for a in a_all]
    x = [eye + a for a in a_blk]
    pw = both(_dot_f32, a_blk, a_blk)
    for _ in range(2):
        x = both(lambda x_, p_: x_ + _dot_f32(x_, p_), x, pw)
        pw = both(_dot_f32, pw, pw)
    x = both(lambda x_, p_: x_ + _dot_f32(x_, p_), x, pw)
    for m in (m_mid, m_top):
        y = both(lambda x_, a_: _dot_f32(x_, a_ * m), x, a_all)
        x = both(lambda x_, y_: x_ + _dot_f32(y_, x_), x, y)
    return x


def _gdn_kernel(q_ref, k_ref, v_ref, gd_ref, grow_ref, m_ref, *rest, rev, finalize):
    if finalize:
        oprev_ref, gate_ref, nw_ref, o_ref, s_ref = rest
    else:
        o_ref, s_ref = rest

    @pl.when(pl.program_id(1) == 0)
    def _():
        s_ref[...] = jnp.zeros_like(s_ref)

    m_incl, m_strict, m_blk, m_mid, m_top, eye = (m_ref[n] for n in range(6))
    units = [(c, h) for c in _chunk_order(rev) for h in range(N_HEAD)]
    rows = lambda c: slice(c * CHUNK, (c + 1) * CHUNK)
    cols = lambda h: slice(h * GDN_DK, (h + 1) * GDN_DK)

    k16 = [k_ref[rows(c), cols(h)] for c, h in units]
    q = [q_ref[rows(c), cols(h)].astype(F32) for c, h in units]
    k = [t.astype(F32) for t in k16]
    v = [v_ref[rows(c), cols(h)].astype(F32) for c, h in units]
    g_col = [gd_ref[rows(c), h:h + 1] for c, h in units]
    beta = [gd_ref[rows(c), N_HEAD + h:N_HEAD + h + 1] for c, h in units]
    g_row = [grow_ref[c, h:h + 1, :] for c, h in units]
    g_tot = [g[0:1] if rev else g[CHUNK - 1:CHUNK] for g in g_col]
    decay = [jnp.exp(jnp.where(m_incl > 0, gc - gr, -jnp.inf)) for gc, gr in zip(g_col, g_row)]
    k_beta = [t * b for t, b in zip(k, beta)]
    a = [-(_dot_nt(kb.astype(BF16), kk) * d) * m_strict for kb, kk, d in zip(k_beta, k16, decay)]
    t16 = [(x - eye).astype(BF16) for x in _unit_tri_inverse(a, m_blk, m_mid, m_top, eye)]
    v_beta = [t * b for t, b in zip(v, beta)]
    e_col = [jnp.exp(g) for g in g_col]
    kbg = [kb * e for kb, e in zip(k_beta, e_col)]
    u = [vb + _dot(t, vb.astype(BF16)) for t, vb in zip(t16, v_beta)]
    w16 = [(x + _dot(t, x.astype(BF16))).astype(BF16) for t, x in zip(t16, kbg)]
    sc16 = [(_dot_nt(qq.astype(BF16), kk) * d).astype(BF16) for qq, kk, d in zip(q, k16, decay)]
    q_in = [(qq * e).astype(BF16) for qq, e in zip(q, e_col)]
    k_out = [(kk * jnp.exp(gt - gc)).astype(BF16) for kk, gt, gc in zip(k, g_tot, g_col)]
    a_end = [jnp.exp(gt) for gt in g_tot]

    s = [s_ref[h] for h in range(N_HEAD)]
    for ci in range(CPB):
        us = range(ci * N_HEAD, (ci + 1) * N_HEAD)
        s16 = [t.astype(BF16) for t in s]
        vn16 = [(u[n] - _dot(w16[n], s16[h])).astype(BF16) for h, n in enumerate(us)]
        o = [_dot(q_in[n], s16[h]) + _dot(sc16[n], vn16[h]) for h, n in enumerate(us)]
        s = [a_end[n] * s[h] + _dot_tn(k_out[n], vn16[h]) for h, n in enumerate(us)]
        for h, n in enumerate(us):
            c = units[n][0]
            oh = o[h]
            if finalize:
                oh = _finalize(oh, oprev_ref, gate_ref, nw_ref, c * CHUNK, h * GDN_DV, GDN_DV)
            o_ref[rows(c), cols(h)] = oh.astype(o_ref.dtype)
    for h in range(N_HEAD):
        s_ref[h] = s[h]


PREP_SLAB = 512


def _conv_prep_kernel(z_ref, w_ref, b_ref, o_ref, *, gdn):
    L = z_ref.shape[0]
    slab = min(PREP_SLAB, L)
    cb = pl.program_id(1)
    w0, w1, w2 = w_ref[0:1, :], w_ref[1:2, :], w_ref[2:3, :]
    row = lax.broadcasted_iota(jnp.int32, (slab, z_ref.shape[1]), 0)
    zero_row = jnp.zeros((1, z_ref.shape[1]), F32)
    for r0 in range(0, L, slab):
        z = z_ref[r0:r0 + slab, :].astype(F32)
        halo = 16
        before = z_ref[r0 - halo:r0, :].astype(F32)[halo - 1:halo] if r0 > 0 else zero_row
        after = z_ref[r0 + slab:r0 + slab + halo, :].astype(F32)[0:1] if r0 + slab < L else zero_row
        prev = jnp.where(row == 0, before, pltpu.roll(z, 1, 0))
        nxt = jnp.where(row == slab - 1, after, pltpu.roll(z, slab - 1, 0))
        y = prev * w0 + z * w1 + nxt * w2
        if not gdn:
            o_ref[r0:r0 + slab, :] = (y + b_ref[...]).astype(o_ref.dtype)
            continue
        y = _silu(y)

        @pl.when(cb < 2)
        def _():
            scale = jnp.where(cb == 0, GDN_DK ** -0.5, 1.0)
            for h in range(N_HEAD):
                t = y[:, h * GDN_DK:(h + 1) * GDN_DK]
                t = t * (lax.rsqrt(jnp.sum(t * t, axis=-1, keepdims=True) + EPS) * scale)
                o_ref[r0:r0 + slab, h * GDN_DK:(h + 1) * GDN_DK] = t.astype(o_ref.dtype)

        @pl.when(cb == 2)
        def _():
            o_ref[r0:r0 + slab, :] = y.astype(o_ref.dtype)


def _conv_prep(p, col0, conv_w, conv_b, n_seq, seq_len, gdn):
    T = p.shape[0]
    dtype = BF16 if gdn else F32
    bias = jnp.zeros((1, 3 * COLW), F32) if conv_b is None else conv_b.reshape(1, -1)
    outs = []
    for L, row0 in ((seq_len, 0), (CTX_LEN, n_seq * seq_len // CTX_LEN)):
        outs.append(pl.pallas_call(
            functools.partial(_conv_prep_kernel, gdn=gdn),
            grid=(n_seq, 3),
            in_specs=[pl.BlockSpec((L, COLW), lambda b, c, row0=row0: (row0 + b, col0 + c)),
                      pl.BlockSpec((3, COLW), lambda b, c: (0, c)),
                      pl.BlockSpec((1, COLW), lambda b, c: (0, c))],
            out_specs=pl.BlockSpec((L, COLW), lambda b, c: (b, c)),
            out_shape=jax.ShapeDtypeStruct((n_seq * L, 3 * COLW), dtype),
            compiler_params=_cparams(("parallel", "parallel")),
            name="conv_prep_%s_L%d" % ("gdn" if gdn else "hyena", L),
        )(p, conv_w, bias))
    return outs


def _gdn(p, p_small, conv_w, a_log, dt_bias, norm_w, n_seq, n_lat_blk):
    T = p.shape[0]
    qkv = jnp.concatenate(_conv_prep(p, CB_GDN_Q, conv_w, None, n_seq, n_lat_blk * SCAN_BLK, True),
                          axis=0)
    a = p_small[:, SM_A:SM_A + 2 * N_HEAD].reshape(T, 2, N_HEAD)
    b = p_small[:, SM_B:SM_B + 2 * N_HEAD].reshape(T, 2, N_HEAD)
    g = (-jnp.exp(a_log) * jax.nn.softplus(a + dt_bias)).reshape(T // CHUNK, CHUNK, 2, N_HEAD)
    beta = jax.nn.sigmoid(b)
    out = None
    for rev in (True, False):
        finalize = not rev
        d = 1 if rev else 0
        gd_ = g[:, :, d]
        G = jnp.flip(jnp.cumsum(jnp.flip(gd_, 1), axis=1), 1) if rev else jnp.cumsum(gd_, axis=1)
        gd = jnp.concatenate([G.reshape(T, N_HEAD), beta[:, d]], axis=-1)
        grow = jnp.transpose(G, (0, 2, 1))
        rows = _scan_rows(n_seq, n_lat_blk, rev)
        extra = [(gd, pl.BlockSpec((SCAN_BLK, 2 * N_HEAD), lambda b_, j, rows=rows: (rows(b_, j), 0))),
                 (grow, pl.BlockSpec((CPB, N_HEAD, CHUNK), lambda b_, j, rows=rows: (rows(b_, j), 0, 0)))]
        kern = functools.partial(_gdn_kernel, rev=rev, finalize=finalize)
        out = _scan_call(kern, "gdn_bwd" if rev else "gdn_fwd", n_seq, n_lat_blk, rev, finalize,
                         [(qkv, 0), (qkv, 1), (qkv, 2)], [], [_gdn_consts(rev)], extra,
                         out, (p, CB_GDN_GATE), norm_w,
                         [pltpu.VMEM((N_HEAD, GDN_DK, GDN_DV), F32)])
    return out


GLA_SUB = 4


def _gla_consts(rev):
    C = CHUNK
    idx = np.arange(C)
    before = _before(rev)
    incl = before | np.eye(C, dtype=bool)
    sizes = (16, 4, 1)
    blk = [idx // s for s in sizes]
    par = [idx // 64, idx // 16, idx // 4]
    sub = [b % GLA_SUB for b in blk]

    def blk_before(l, a, b):
        return (blk[l][a] > blk[l][b]) if rev else (blk[l][a] < blk[l][b])

    pq = np.zeros((2, C, C), np.float32)
    for l in range(2):
        pq[l] = incl & (blk[l][:, None] == blk[l][None, :])
    kmt = np.zeros((3, C, GLA_SUB * C), np.float32)
    maskx = np.zeros((4, C, GLA_SUB * C), np.float32)
    for l in range(3):
        for m in range(GLA_SUB):
            for j in range(C):
                tgt = par[l][j] * GLA_SUB + m
                valid = (tgt < blk[l][j]) if rev else (tgt > blk[l][j])
                if not valid:
                    continue
                t_after_j = before[:, j]
                t_blk_before_tgt = (blk[l] > tgt) if rev else (blk[l] < tgt)
                kmt[l, :, m * C + j] = t_after_j & t_blk_before_tgt
            for i in range(C):
                if sub[l][i] != m:
                    continue
                sel = (par[l] == par[l][i]) & np.array([blk_before(l, j, i) for j in range(C)])
                maskx[l, i, m * C:(m + 1) * C] = sel
    maskx[3, :, :C] = np.eye(C)
    id4 = np.tile(np.eye(C, dtype=np.float32), (1, GLA_SUB))
    return (jnp.asarray(incl.astype(np.float32), BF16), jnp.asarray(pq, BF16), jnp.asarray(kmt, BF16),
            jnp.asarray(maskx, F32), jnp.asarray(id4, BF16))


def _gla_kernel(qk_ref, v_ref, sm_ref, w2_ref, b2_ref, incl_ref, pq_ref, kmt_ref, mx_ref, id4_ref, *rest,
                rev, finalize):
    if finalize:
        oprev_ref, gate_ref, nw_ref, o_ref, st_ref = rest
    else:
        o_ref, st_ref = rest

    @pl.when(pl.program_id(1) == 0)
    def _():
        st_ref[...] = jnp.zeros_like(st_ref)

    hw = N_HEAD * GLA_DK
    x = _dot(sm_ref[...].astype(BF16), w2_ref[...]) + b2_ref[...]
    gk_all = (jnp.minimum(x, 0.0) - jnp.log(1.0 + jnp.exp(-jnp.abs(x)))) * (1.0 / GLA_TAU)
    incl = incl_ref[...]
    id4 = id4_ref[...]
    order = list(_chunk_order(rev))
    units = [(ci, h) for ci in range(CPB) for h in range(N_HEAD)]
    rows = lambda c: slice(c * CHUNK, (c + 1) * CHUNK)
    kcols = lambda h: slice(h * GLA_DK, (h + 1) * GLA_DK)
    vcols = lambda h: slice(h * GLA_DV, (h + 1) * GLA_DV)

    gk = [gk_all[rows(c)] for c in order]
    q = [qk_ref[rows(c), :hw].astype(F32) * GLA_DK ** -0.5 for c in order]
    k = [qk_ref[rows(c), hw:].astype(F32) for c in order]
    vb = [v_ref[rows(c), :].astype(BF16) for c in order]
    G = [_dot_const(incl, g, 3) for g in gk]
    g_tot = [g[0:1] if rev else g[CHUNK - 1:CHUNK] for g in G]
    e1 = [_dot_const(pq_ref[0], g, 3) for g in gk]
    e2 = [_dot_const(pq_ref[1], g, 3) for g in gk]
    q_lvl = [[(qq * jnp.exp(e)).astype(BF16) for qq, e in zip(q, es)] for es in (e1, e2, gk)]
    q_lvl.append([qq.astype(BF16) for qq in q])
    q_in = [(qq * jnp.exp(g)).astype(BF16) for qq, g in zip(q, G)]
    k_out = [(kk * jnp.exp(gt - g)).astype(BF16) for kk, gt, g in zip(k, g_tot, G)]
    kb = [kk.astype(BF16) for kk in k]
    decay_tot = [jnp.exp(gt) for gt in g_tot]

    kt = [_dot_tn(kb[ci][:, kcols(h)], id4) for ci, h in units]
    sx = [_dot(q_lvl[3][ci][:, kcols(h)], t.astype(BF16)) * mx_ref[3] for (ci, h), t in zip(units, kt)]
    for l in range(3):
        e = [_dot_tn_const(gk[ci][:, kcols(h)], kmt_ref[l], 2) for ci, h in units]
        kt_l = [(t * jnp.exp(e_)).astype(BF16) for t, e_ in zip(kt, e)]
        sx = [s_ + _dot(q_lvl[l][ci][:, kcols(h)], t) * mx_ref[l] for (ci, h), s_, t in zip(units, sx, kt_l)]
    vh = [vb[ci][:, vcols(h)] for ci, h in units]
    o_intra = [_dot(s_.astype(BF16), jnp.concatenate([v_] * GLA_SUB, axis=0)) for s_, v_ in zip(sx, vh)]
    kv = [_dot_tn(v_, k_out[ci][:, kcols(h)]) for (ci, h), v_ in zip(units, vh)]

    st = [st_ref[h] for h in range(N_HEAD)]
    for ci in range(CPB):
        c = order[ci]
        for h in range(N_HEAD):
            n = ci * N_HEAD + h
            o = o_intra[n] + _dot_nt(q_in[ci][:, kcols(h)], st[h].astype(BF16))
            st[h] = st[h] * decay_tot[ci][:, kcols(h)] + kv[n]
            if finalize:
                o = _finalize(o, oprev_ref, gate_ref, nw_ref, c * CHUNK, h * GLA_DV, GLA_DV)
            o_ref[rows(c), vcols(h)] = o.astype(o_ref.dtype)
    for h in range(N_HEAD):
        st_ref[h] = st[h]


def _gla(p, p_small, w2, b2, norm_w, n_seq, n_lat_blk):
    out = None
    hw = N_HEAD * GLA_DK
    for rev in (True, False):
        finalize = not rev
        d = 1 if rev else 0
        lr0 = SM_LR + d * GLA_RANK
        w2p = jnp.zeros((V7X_LANES, hw), F32).at[lr0:lr0 + GLA_RANK].set(w2[d]).astype(BF16)
        rows = _scan_rows(n_seq, n_lat_blk, rev)
        extra = [(p_small, pl.BlockSpec((SCAN_BLK, V7X_LANES), lambda b_, j, rows=rows: (rows(b_, j), 0)))]
        kern = functools.partial(_gla_kernel, rev=rev, finalize=finalize)
        out = _scan_call(kern, "gla_bwd" if rev else "gla_fwd", n_seq, n_lat_blk, rev, finalize,
                         [(p, CB_GLA_QK), (p, CB_GLA_V)], [], [w2p, b2[d].reshape(1, hw), *_gla_consts(rev)],
                         extra, out, (p, CB_GLA_GATE), norm_w,
                         [pltpu.VMEM((N_HEAD, GLA_DV, GLA_DK), F32)])
    return out


def _hyena_filters(L, w1, b1, w2, b2, w3, freq):
    t = jnp.linspace(0.0, 1.0, L, dtype=F32)[:, None]
    w = 2.0 * math.pi * jnp.arange(L, dtype=F32) / L
    f = jnp.linspace(1e-4, HY_BANDS - 1, HY_BANDS, dtype=F32)
    ang = w[:, None] * f[None, :]
    z = jnp.concatenate([t, jnp.cos(ang), -jnp.sin(ang)], axis=-1)
    hdn = jnp.sin(freq[0] * (z @ w1 + b1))
    for i in range(HY_INNER):
        hdn = jnp.sin(freq[i + 1] * (hdn @ w2[i] + b2[i]))
    h = (hdn @ w3).reshape(L, HY_ORDER, 2, HY_W)
    deltas = jnp.abs(jnp.linspace(HY_MIN_DECAY, HY_MAX_DECAY, HY_W, dtype=F32))
    h = h * jnp.exp(-t[:, :, None, None] * deltas)
    hf, hb = h[:, :, 0], h[:, :, 1]
    hb = hb.at[0].set(0.0)
    scale = lax.rsqrt(jnp.sum(hf * hf, axis=0) + jnp.sum(hb * hb, axis=0) + EPS)
    return (hf * scale).reshape(L, -1), (hb * scale).reshape(L, -1)


def _dft_mats(L):
    k = jnp.arange(L, dtype=jnp.int32)[:, None]
    n = jnp.arange(L, dtype=jnp.int32)[None, :]
    ang = (((2 * k + 1) * n) % (4 * L)).astype(F32) * (math.pi / (2 * L))
    return tuple(_split_bf16(jnp.cos(ang), 2)) + tuple(_split_bf16(jnp.sin(ang), 2))


def _hy_spectra(hf, hb, dft):
    c_hi, c_lo, s_hi, s_lo = dft

    def mm3(m_hi, m_lo, g):
        g_hi, g_lo = _split_bf16(g, 2)
        return _matmul(m_hi, g_hi) + (_matmul(m_hi, g_lo) + _matmul(m_lo, g_hi))

    return mm3(c_hi, c_lo, hf + hb), mm3(s_hi, s_lo, hb - hf)


def _hy_kernel(z_ref, x_ref, skip_ref, c_ref, s_ref, kr_ref, ki_ref, *rest, last, n_k, slab, inv_scale):
    if last:
        gate_ref, o_ref, zb_ref, acc_ref = rest
    else:
        o_ref, zb_ref, acc_ref = rest
    kt = pl.program_id(2)

    @pl.when(kt == 0)
    def _():
        zb_ref[...] = z_ref[...].astype(BF16)
        acc_ref[...] = jnp.zeros_like(acc_ref)

    c, s = c_ref[...], s_ref[...]
    zb = zb_ref[...]
    zr, zi = _dot(c, zb), -_dot(s, zb)
    kr, ki = kr_ref[...], ki_ref[...]
    pr = (zr * kr - zi * ki).T.astype(BF16)
    pi = (zr * ki + zi * kr).T.astype(BF16)
    acc_ref[...] += _dot(pr, c) - _dot(pi, s)

    @pl.when(kt == n_k - 1)
    def _():
        for r0 in range(0, acc_ref.shape[1], slab):
            conv = acc_ref[:, r0:r0 + slab].T * inv_scale
            z = z_ref[r0:r0 + slab, :]
            y = x_ref[r0:r0 + slab, :] * (conv + skip_ref[...] * z)
            if last:
                y = y * _silu(gate_ref[r0:r0 + slab, :].astype(F32))
            o_ref[r0:r0 + slab, :] = y.astype(o_ref.dtype)


HY_WT = 256


def _hy_order(z_src, x_src, gate_src, skip, dft, kr, ki, order, n_seq, L):
    last = gate_src is not None
    tk = min(512, L)
    n_k = L // tk
    slab = min(512, L)
    n_wt = HY_W // HY_WT

    def rowspec(src):
        _, r0, c0 = src
        return pl.BlockSpec((L, HY_WT), lambda b, w, k: (r0 + b, c0 + w), pipeline_mode=pl.Buffered(1))

    srcs = [z_src, x_src] + ([gate_src] if last else [])
    twid = pl.BlockSpec((tk, L), lambda b, w, k: (k, 0))
    spec = pl.BlockSpec((tk, HY_WT), lambda b, w, k: (k, order * n_wt + w))
    kern = functools.partial(_hy_kernel, last=last, n_k=n_k, slab=slab, inv_scale=1.0 / L)
    args = [z_src[0], x_src[0], skip[order].reshape(1, HY_W), dft[0], dft[2], kr, ki]
    specs = [rowspec(z_src), rowspec(x_src), pl.BlockSpec((1, HY_WT), lambda b, w, k: (0, w)),
             twid, twid, spec, spec]
    if last:
        args.append(gate_src[0])
        specs.append(rowspec(gate_src))
    return pl.pallas_call(
        kern,
        grid=(n_seq, n_wt, n_k),
        in_specs=specs,
        out_specs=pl.BlockSpec((L, HY_WT), lambda b, w, k: (b, w)),
        out_shape=jax.ShapeDtypeStruct((n_seq * L, HY_W), BF16 if last else F32),
        scratch_shapes=[pltpu.VMEM((L, HY_WT), BF16), pltpu.VMEM((HY_WT, L), F32)],
        compiler_params=_cparams(("parallel", "parallel", "arbitrary")),
        name="hyena_order%d_L%d" % (order, L),
    )(*args)


def _hyena(p, conv_w, conv_b, w1, b1, w2, b2, w3, freq, skip, dfts, n_seq, seq_len):
    zcs = _conv_prep(p, CB_HY_V, conv_w, conv_b, n_seq, seq_len, False)
    n_wt = HY_W // HY_WT
    outs = []
    for zc, L, row0 in zip(zcs, (seq_len, CTX_LEN), (0, n_seq * seq_len // CTX_LEN)):
        dft = dfts[L]
        kr, ki = _hy_spectra(*_hyena_filters(L, w1, b1, w2, b2, w3, freq), dft)
        y1 = _hy_order((zc, 0, 0), (zc, 0, n_wt), None, skip, dft, kr, ki, 0, n_seq, L)
        outs.append(_hy_order((y1, 0, 0), (zc, 0, 2 * n_wt), (p, row0, CB_HY_GATE * n_wt),
                              skip, dft, kr, ki, 1, n_seq, L))
    return jnp.concatenate(outs, axis=0)


def _layout_w_in(w_in):
    sizes = [3 * HY_W, HY_W, 3 * BR_W, 2 * N_HEAD, 2 * N_HEAD, BR_W, 2 * N_HEAD * RET_DK, BR_W, BR_W,
             2 * N_HEAD * GLA_DK, BR_W, 2 * GLA_RANK, BR_W]
    offs = np.concatenate([[0], np.cumsum(sizes)])
    (hy_proj, hy_gate, gdn_qkv, gdn_a, gdn_b, gdn_gate, ret_qk, ret_v, ret_gate,
     gla_qk, gla_v, gla_lr, gla_gate) = [w_in[:, offs[n]:offs[n + 1]] for n in range(len(sizes))]
    D = w_in.shape[0]
    half = RET_DK // 2
    ret_rot = ret_qk.reshape(D, 2 * N_HEAD, 2, half)[:, :, ::-1].reshape(D, -1)
    main = jnp.concatenate([hy_proj, hy_gate, gdn_qkv, gdn_gate, ret_qk, ret_rot, ret_v, ret_gate,
                            gla_qk, gla_v, gla_gate], axis=1)
    small = jnp.concatenate([gdn_a, gdn_b, gla_lr], axis=1)
    small = jnp.pad(small, ((0, 0), (0, V7X_LANES - small.shape[1])))
    return main, small


def _layer(x, mod, dims, rope, dfts, norm_w, w_in, hy_conv_w, hy_conv_b, hy_w1, hy_b1, hy_w2, hy_b2, hy_w3,
           hy_freq, hy_skip, gdn_conv_w, gdn_a_log, gdn_dt_bias, gdn_norm, ret_norm, gla_w2, gla_b2,
           gla_norm, w_branch, w_merge, b_merge, w_out):
    n_seq, seq_len = dims
    n_lat_rows = n_seq * seq_len
    n_lat_blk = seq_len // SCAN_BLK
    shift, scale, gate = mod
    w_main, w_small = _layout_w_in(w_in.astype(BF16))
    p, h = _inproj(x, norm_w, scale, shift, w_main, n_lat_rows, seq_len)
    p_small = _matmul(h, w_small)

    y_hy = _hyena(p, hy_conv_w, hy_conv_b, hy_w1, hy_b1, hy_w2, hy_b2, hy_w3, hy_freq, hy_skip, dfts,
                  n_seq, seq_len)
    y_gdn = _gdn(p, p_small, gdn_conv_w, gdn_a_log, gdn_dt_bias, gdn_norm, n_seq, n_lat_blk)
    y_ret = _retention(p, rope[0], rope[1], ret_norm, n_seq, n_lat_blk)
    y_gla = _gla(p, p_small, gla_w2, gla_b2, gla_norm, n_seq, n_lat_blk)

    merged = _merge(h, (y_hy, y_gdn, y_ret, y_gla), w_merge.astype(BF16), b_merge, w_branch.astype(BF16))
    return _outproj(merged, w_out.astype(BF16), x, gate, n_lat_rows, seq_len)


def kernel(x, c, ctx, c_ctx, norm_w, ada_w, ada_b, w_in, hy_conv_w, hy_conv_b, hy_w1, hy_b1,
           hy_w2, hy_b2, hy_w3, hy_freq, hy_skip, gdn_conv_w, gdn_a_log, gdn_dt_bias,
           gdn_norm, ret_norm, gla_w2, gla_b2, gla_norm, w_branch, w_merge, b_merge, w_out,
           final_norm):
    B, S, D = x.shape
    layer_params = (norm_w, w_in, hy_conv_w, hy_conv_b, hy_w1, hy_b1, hy_w2, hy_b2, hy_w3, hy_freq,
                    hy_skip, gdn_conv_w, gdn_a_log, gdn_dt_bias, gdn_norm, ret_norm, gla_w2, gla_b2,
                    gla_norm, w_branch, w_merge, b_merge, w_out)
    xt = jnp.concatenate([x.reshape(-1, D), ctx.reshape(-1, D)], axis=0)
    cond = _silu(jnp.concatenate([c_ctx[None], c], axis=0))
    cond = jnp.pad(cond, ((0, -(1 + B) % 8), (0, 0))).astype(BF16)
    rope = _rope_tables(S)
    dfts = {L: _dft_mats(L) for L in (S, CTX_LEN)}
    for l in range(DEPTH):
        m = (_matmul(cond, ada_w[l].astype(BF16)) + ada_b[l])[:1 + B]
        mod = tuple(t.reshape(1 + B, 1, D) for t in jnp.split(m, 3, axis=-1))
        xt = _layer(xt, mod, (B, S), rope, dfts, *[p[l] for p in layer_params])
    return _final_norm(xt, final_norm, B * S).reshape(B, S, D)
```

```python
import functools
import math

import numpy as np
import jax
import jax.numpy as jnp
from jax import lax
from jax.experimental import pallas as pl
from jax.experimental.pallas import tpu as pltpu

D_MODEL = 2048
DEPTH = 4
CTX_LEN = 256
GRID_W = 64
F32 = jnp.float32
BF16 = jnp.bfloat16
EPS = 1e-6
CHUNK = 64

N_BRANCH = 4
BR_W = D_MODEL // N_BRANCH

HY_W = BR_W
HY_ORDER = 2
HY_EMB = 33
HY_BANDS = (HY_EMB - 1) // 2
HY_INNER = 2
HY_FAST_DECAY = 0.3
HY_SLOW_DECAY = 1.5
HY_TARGET = 1e-2
HY_MIN_DECAY = math.log(HY_TARGET) / HY_SLOW_DECAY
HY_MAX_DECAY = math.log(HY_TARGET) / HY_FAST_DECAY

N_HEAD = 4
GDN_DK = BR_W // N_HEAD
GDN_DV = BR_W // N_HEAD
RET_DK = BR_W // (2 * N_HEAD)
RET_DV = BR_W // N_HEAD
RET_ROPE_PAIRS = (8, 12, 12)
ROPE_BASE = 10000.0
GLA_DK = BR_W // (2 * N_HEAD)
GLA_DV = BR_W // N_HEAD
GLA_RANK = 16
GLA_TAU = 16.0

V7X_LANES = 128
VMEM_LIMIT_BYTES = 56 * 1024 * 1024

SCAN_BLK = CTX_LEN
CPB = SCAN_BLK // CHUNK
COLW = BR_W

CB_HY_V, CB_HY_X1, CB_HY_X2, CB_HY_GATE = 0, 1, 2, 3
CB_GDN_Q, CB_GDN_GATE = 4, 7
CB_RET_QK, CB_RET_QKROT, CB_RET_V, CB_RET_GATE = 8, 9, 10, 11
CB_GLA_QK, CB_GLA_V, CB_GLA_GATE = 12, 13, 14
N_COLBLK = 15
SM_A, SM_B, SM_LR = 0, 2 * N_HEAD, 4 * N_HEAD


def _dot(a, b):
    return jnp.dot(a, b, preferred_element_type=F32)


def _dot_nt(a, b):
    return lax.dot_general(a, b, (((1,), (1,)), ((), ())), preferred_element_type=F32)


def _dot_tn(a, b):
    return lax.dot_general(a, b, (((0,), (0,)), ((), ())), preferred_element_type=F32)


def _split_bf16(x, terms):
    parts = []
    r = x
    for t in range(terms):
        p = r.astype(BF16)
        parts.append(p)
        if t + 1 < terms:
            r = r - p.astype(F32)
    return parts


def _dot_f32(a, b):
    a1, a2 = _split_bf16(a, 2)
    b1, b2 = _split_bf16(b, 2)
    return _dot(a1, b1) + (_dot(a1, b2) + _dot(a2, b1))


def _dot_const(c, x, terms):
    out = None
    for p in _split_bf16(x, terms):
        t = _dot(c, p)
        out = t if out is None else out + t
    return out


def _dot_tn_const(x, c, terms):
    out = None
    for p in _split_bf16(x, terms):
        t = _dot_tn(p, c)
        out = t if out is None else out + t
    return out


def _silu(x):
    return x * jax.nn.sigmoid(x)


def _cparams(sem):
    return pltpu.CompilerParams(dimension_semantics=sem, vmem_limit_bytes=VMEM_LIMIT_BYTES)


def _mm_kernel(a_ref, b_ref, o_ref):
    o_ref[...] = _dot(a_ref[...], b_ref[...]).astype(o_ref.dtype)


def _pick_tile(n, cands):
    for c in cands:
        if n % c == 0:
            return c
    return n


def _matmul(a, b, out_dtype=F32):
    M, K = a.shape
    _, N = b.shape
    tm = _pick_tile(M, (1024, 512, 256, 128, 64, 32, 16, 8))
    tn = _pick_tile(N, (512, 384, 256, 128))
    return pl.pallas_call(
        _mm_kernel,
        grid=(M // tm, N // tn),
        in_specs=[pl.BlockSpec((tm, K), lambda i, j: (i, 0)),
                  pl.BlockSpec((K, tn), lambda i, j: (0, j))],
        out_specs=pl.BlockSpec((tm, tn), lambda i, j: (i, j)),
        out_shape=jax.ShapeDtypeStruct((M, N), out_dtype),
        compiler_params=_cparams(("parallel", "arbitrary")),
        name="matmul",
    )(a, b)


def _mod_row(n_lat_blk, blk_per_seq):
    return lambda i: jnp.where(i < n_lat_blk, 1 + i // blk_per_seq, 0)


def _inproj_kernel(x_ref, nw_ref, sc_ref, sh_ref, w_ref, p_ref, h_ref, hs_ref):
    @pl.when(pl.program_id(1) == 0)
    def _():
        for r0 in range(0, x_ref.shape[0], PREP_SLAB):
            x = x_ref[r0:r0 + PREP_SLAB, :]
            y = x * lax.rsqrt(jnp.mean(x * x, axis=-1, keepdims=True) + EPS) * nw_ref[...]
            h = (y * (1.0 + sc_ref[0]) + sh_ref[0]).astype(BF16)
            hs_ref[r0:r0 + PREP_SLAB, :] = h
            h_ref[r0:r0 + PREP_SLAB, :] = h

    p_ref[...] = _dot(hs_ref[...], w_ref[...]).astype(p_ref.dtype)


def _inproj(x, norm_w, scale, shift, w, n_lat_rows, seq_len):
    T, D = x.shape
    N = w.shape[1]
    tm, tn = 1024, 3 * COLW
    row = _mod_row(n_lat_rows // tm, seq_len // tm)
    return pl.pallas_call(
        _inproj_kernel,
        grid=(T // tm, N // tn),
        in_specs=[pl.BlockSpec((tm, D), lambda i, j: (i, 0)),
                  pl.BlockSpec((1, D), lambda i, j: (0, 0)),
                  pl.BlockSpec((1, 1, D), lambda i, j: (row(i), 0, 0)),
                  pl.BlockSpec((1, 1, D), lambda i, j: (row(i), 0, 0)),
                  pl.BlockSpec((D, tn), lambda i, j: (0, j))],
        out_specs=[pl.BlockSpec((tm, tn), lambda i, j: (i, j)),
                   pl.BlockSpec((tm, D), lambda i, j: (i, 0))],
        out_shape=[jax.ShapeDtypeStruct((T, N), BF16), jax.ShapeDtypeStruct((T, D), BF16)],
        scratch_shapes=[pltpu.VMEM((tm, D), BF16)],
        compiler_params=_cparams(("parallel", "arbitrary")),
        name="inproj",
    )(x, norm_w.reshape(1, D), scale, shift, w)


def _merge_kernel(h_ref, y0_ref, y1_ref, y2_ref, y3_ref, wm_ref, bm_ref, wb_ref, o_ref):
    h = h_ref[...]
    acc = None
    for i, y_ref in enumerate((y0_ref, y1_ref, y2_ref, y3_ref)):
        g = jax.nn.sigmoid(_dot(h, wm_ref[i]) + bm_ref[i])
        t = g * _dot(y_ref[...], wb_ref[i])
        acc = t if acc is None else acc + t
    o_ref[...] = acc.astype(o_ref.dtype)


def _merge(h, ys, w_merge, b_merge, w_branch):
    T, D = h.shape
    tm, tn = 1024, 512
    yspec = pl.BlockSpec((tm, BR_W), lambda i, j: (i, 0))
    return pl.pallas_call(
        _merge_kernel,
        grid=(T // tm, D // tn),
        in_specs=[pl.BlockSpec((tm, D), lambda i, j: (i, 0)), yspec, yspec, yspec, yspec,
                  pl.BlockSpec((N_BRANCH, D, tn), lambda i, j: (0, 0, j)),
                  pl.BlockSpec((N_BRANCH, 1, tn), lambda i, j: (0, 0, j)),
                  pl.BlockSpec((N_BRANCH, BR_W, tn), lambda i, j: (0, 0, j))],
        out_specs=pl.BlockSpec((tm, tn), lambda i, j: (i, j)),
        out_shape=jax.ShapeDtypeStruct((T, D), BF16),
        compiler_params=_cparams(("parallel", "arbitrary")),
        name="merge",
    )(h, *ys, w_merge, b_merge.reshape(N_BRANCH, 1, D), w_branch)


def _outproj_kernel(a_ref, w_ref, x_ref, g_ref, o_ref):
    o_ref[...] = x_ref[...] + g_ref[0] * _dot(a_ref[...], w_ref[...])


def _outproj(a, w, x, gate, n_lat_rows, seq_len):
    T, D = x.shape
    tm, tn = 1024, 1024
    row = _mod_row(n_lat_rows // tm, seq_len // tm)
    return pl.pallas_call(
        _outproj_kernel,
        grid=(T // tm, D // tn),
        in_specs=[pl.BlockSpec((tm, D), lambda i, j: (i, 0)),
                  pl.BlockSpec((D, tn), lambda i, j: (0, j)),
                  pl.BlockSpec((tm, tn), lambda i, j: (i, j)),
                  pl.BlockSpec((1, 1, tn), lambda i, j: (row(i), 0, j))],
        out_specs=pl.BlockSpec((tm, tn), lambda i, j: (i, j)),
        out_shape=jax.ShapeDtypeStruct((T, D), F32),
        compiler_params=_cparams(("parallel", "arbitrary")),
        name="outproj",
    )(a, w, x, gate)


def _final_norm_kernel(x_ref, w_ref, o_ref):
    x = x_ref[...]
    o_ref[...] = x * lax.rsqrt(jnp.mean(x * x, axis=-1, keepdims=True) + EPS) * w_ref[...]


def _final_norm(x, w, n_lat_rows):
    D = x.shape[1]
    tm = 512
    return pl.pallas_call(
        _final_norm_kernel,
        grid=(n_lat_rows // tm,),
        in_specs=[pl.BlockSpec((tm, D), lambda i: (i, 0)),
                  pl.BlockSpec((1, D), lambda i: (0, 0))],
        out_specs=pl.BlockSpec((tm, D), lambda i: (i, 0)),
        out_shape=jax.ShapeDtypeStruct((n_lat_rows, D), F32),
        compiler_params=_cparams(("parallel",)),
        name="final_norm",
    )(x, w.reshape(1, D))


def _scan_rows(n_seq, n_lat_blk, rev):
    def idx(b, j):
        jj = (n_lat_blk - j) if rev else (j - 1)
        return jnp.where(j == 0, n_seq * n_lat_blk + b, b * n_lat_blk + jj)
    return idx


def _chunk_order(rev):
    return range(CPB - 1, -1, -1) if rev else range(CPB)


def _before(rev):
    i = np.arange(CHUNK)[:, None]
    j = np.arange(CHUNK)[None, :]
    return (j > i) if rev else (j < i)


def _finalize(o, oprev_ref, gate_ref, nw_ref, r0, c0, width):
    o = o + oprev_ref[r0:r0 + CHUNK, c0:c0 + width]
    o = o * lax.rsqrt(jnp.mean(o * o, axis=-1, keepdims=True) + EPS) * nw_ref[...]
    return o * _silu(gate_ref[r0:r0 + CHUNK, c0:c0 + width].astype(F32))


def _scan_call(kernel, name, n_seq, n_lat_blk, rev, finalize, row_inputs, tab_inputs, const_inputs,
               extra_specs_inputs, oprev, gate_src, norm_w, scratch):
    rows = _scan_rows(n_seq, n_lat_blk, rev)
    T = row_inputs[0][0].shape[0]
    args, specs = [], []
    for arr, cb in row_inputs:
        args.append(arr)
        specs.append(pl.BlockSpec((SCAN_BLK, COLW), functools.partial(
            lambda b, j, cb: (rows(b, j), cb), cb=cb)))
    for arr in tab_inputs:
        args.append(arr)
        specs.append(pl.BlockSpec((SCAN_BLK, arr.shape[1]), lambda b, j: (
            jnp.where(j == 0, 0, 1 + ((n_lat_blk - j) if rev else (j - 1))), 0)))
    for arr, spec in extra_specs_inputs:
        args.append(arr)
        specs.append(spec)
    for arr in const_inputs:
        args.append(arr)
        specs.append(pl.BlockSpec(arr.shape, functools.partial(lambda b, j, n: (0,) * n, n=arr.ndim)))
    if finalize:
        args += [oprev, gate_src[0], norm_w.reshape(1, -1)]
        specs += [pl.BlockSpec((SCAN_BLK, COLW), lambda b, j: (rows(b, j), 0)),
                  pl.BlockSpec((SCAN_BLK, COLW), functools.partial(
                      lambda b, j, cb: (rows(b, j), cb), cb=gate_src[1])),
                  pl.BlockSpec((1, norm_w.shape[0]), lambda b, j: (0, 0))]
    return pl.pallas_call(
        kernel,
        grid=(n_seq, 1 + n_lat_blk),
        in_specs=specs,
        out_specs=pl.BlockSpec((SCAN_BLK, COLW), lambda b, j: (rows(b, j), 0)),
        out_shape=jax.ShapeDtypeStruct((T, COLW), BF16 if finalize else F32),
        scratch_shapes=scratch,
        compiler_params=_cparams(("parallel", "arbitrary")),
        name=name,
    )(*args)


def _ret_consts(rev):
    lg = np.log(1.0 - np.power(2.0, -5.0 - np.arange(N_HEAD, dtype=np.float64)))
    i = np.arange(CHUNK, dtype=np.float64)
    steps = (CHUNK - i) if rev else (i + 1.0)
    G = steps[None, :] * lg[:, None]
    mask = _before(rev) | np.eye(CHUNK, dtype=bool)
    dm = np.where(mask[None], np.exp(G[:, :, None] - G[:, None, :]), 0.0)
    qs = np.repeat(np.exp(G).T, RET_DK, axis=1)
    ks = np.repeat(np.exp(CHUNK * lg[:, None] - G).T, RET_DK, axis=1)
    a_end = tuple(float(v) for v in np.exp(CHUNK * lg))
    return (jnp.asarray(dm, F32), jnp.asarray(qs, F32), jnp.asarray(ks, F32)), a_end


def _ret_kernel(qk_ref, qkr_ref, v_ref, cos_ref, sin_ref, dm_ref, qs_ref, ks_ref, *rest,
                rev, finalize, a_end):
    if finalize:
        oprev_ref, gate_ref, nw_ref, o_ref, s_ref = rest
    else:
        o_ref, s_ref = rest

    @pl.when(pl.program_id(1) == 0)
    def _():
        s_ref[...] = jnp.zeros_like(s_ref)

    hw = N_HEAD * RET_DK
    qk = (qk_ref[...].astype(F32) * cos_ref[...]
          + qkr_ref[...].astype(F32) * sin_ref[...])
    order = list(_chunk_order(rev))
    units = [(ci, h) for ci in range(CPB) for h in range(N_HEAD)]
    rows = lambda c: slice(c * CHUNK, (c + 1) * CHUNK)
    kcols = lambda h: slice(h * RET_DK, (h + 1) * RET_DK)
    vcols = lambda h: slice(h * RET_DV, (h + 1) * RET_DV)

    qc = [qk[rows(c), :hw] for c in order]
    kc = [qk[rows(c), hw:] for c in order]
    qb = [t.astype(BF16) for t in qc]
    kb = [t.astype(BF16) for t in kc]
    q_in = [(t * qs_ref[...]).astype(BF16) for t in qc]
    k_out = [(t * ks_ref[...]).astype(BF16) for t in kc]
    vb = [v_ref[rows(c), :].astype(BF16) for c in order]
    vh = [vb[ci][:, vcols(h)] for ci, h in units]
    sc = [(_dot_nt(qb[ci][:, kcols(h)], kb[ci][:, kcols(h)]) * dm_ref[h]).astype(BF16) for ci, h in units]
    o_intra = [_dot(s_, v_) for s_, v_ in zip(sc, vh)]
    kv = [_dot_tn(k_out[ci][:, kcols(h)], v_) for (ci, h), v_ in zip(units, vh)]

    s = [s_ref[h] for h in range(N_HEAD)]
    for ci in range(CPB):
        c = order[ci]
        for h in range(N_HEAD):
            n = ci * N_HEAD + h
            o = o_intra[n] + _dot(q_in[ci][:, kcols(h)], s[h].astype(BF16))
            s[h] = a_end[h] * s[h] + kv[n]
            if finalize:
                o = _finalize(o, oprev_ref, gate_ref, nw_ref, c * CHUNK, h * RET_DV, RET_DV)
            o_ref[rows(c), vcols(h)] = o.astype(o_ref.dtype)
    for h in range(N_HEAD):
        s_ref[h] = s[h]


def _retention(p, cos_tab, sin_tab, norm_w, n_seq, n_lat_blk):
    out = None
    for rev in (True, False):
        finalize = not rev
        consts, a_end = _ret_consts(rev)
        kern = functools.partial(_ret_kernel, rev=rev, finalize=finalize, a_end=a_end)
        out = _scan_call(kern, "retention_bwd" if rev else "retention_fwd", n_seq, n_lat_blk, rev, finalize,
                         [(p, CB_RET_QK), (p, CB_RET_QKROT), (p, CB_RET_V)], [cos_tab, sin_tab],
                         list(consts), [], out, (p, CB_RET_GATE), norm_w,
                         [pltpu.VMEM((N_HEAD, RET_DK, RET_DV), F32)])
    return out


def _rope_tables(seq_len):
    def angles(pos):
        angs = []
        for a, n in enumerate(RET_ROPE_PAIRS):
            freqs = ROPE_BASE ** (-jnp.arange(n, dtype=F32) / n)
            angs.append(pos[:, a:a + 1] * freqs)
        return jnp.concatenate(angs, axis=-1)

    t = jnp.arange(CTX_LEN, dtype=F32)
    zero = jnp.zeros((CTX_LEN,), F32)
    pos_c = jnp.stack([t, zero, zero], axis=-1)
    rows = seq_len // GRID_W
    r = jnp.repeat(jnp.arange(rows, dtype=F32), GRID_W)
    col = jnp.broadcast_to(jnp.arange(GRID_W, dtype=F32), (rows, GRID_W)).reshape(-1)
    pos_l = jnp.stack([jnp.full((seq_len,), CTX_LEN, F32), r, col], axis=-1)
    ang = jnp.concatenate([angles(pos_c), angles(pos_l)], axis=0)
    cos = jnp.concatenate([jnp.cos(ang), jnp.cos(ang)], axis=-1)
    sin = jnp.concatenate([-jnp.sin(ang), jnp.sin(ang)], axis=-1)
    scale = jnp.concatenate([jnp.ones((N_HEAD * RET_DK,), F32),
                             jnp.full((N_HEAD * RET_DK,), RET_DK ** -0.5, F32)])
    return jnp.tile(cos, (1, 2 * N_HEAD)) * scale, jnp.tile(sin, (1, 2 * N_HEAD)) * scale


def _gdn_consts(rev):
    i = np.arange(CHUNK)
    before = _before(rev)
    eye = np.eye(CHUNK, dtype=bool)
    same16 = (i[:, None] // 16) == (i[None, :] // 16)
    same32 = (i[:, None] // 32) == (i[None, :] // 32)
    masks = np.stack([before | eye, before, before & same16, before & same32 & ~same16,
                      before & ~same32, eye]).astype(np.float32)
    return jnp.asarray(masks)


def _unit_tri_inverse(a_all, m_blk, m_mid, m_top, eye):
    both = lambda f, xs, ys: [f(x, y) for x, y in zip(xs, ys)]
    a_blk = [a * m_blk for a in a_all]
    x = [eye + a for a in a_blk]
    pw = both(_dot_f32, a_blk, a_blk)
    for _ in range(2):
        x = both(lambda x_, p_: x_ + _dot_f32(x_, p_), x, pw)
        pw = both(_dot_f32, pw, pw)
    x = both(lambda x_, p_: x_ + _dot_f32(x_, p_), x, pw)
    for m in (m_mid, m_top):
        y = both(lambda x_, a_: _dot_f32(x_, a_ * m), x, a_all)
        x = both(lambda x_, y_: x_ + _dot_f32(y_, x_), x, y)
    return x


def _gdn_kernel(q_ref, k_ref, v_ref, sm_ref, ea_ref, dtb_ref, sel_ref, m_ref, *rest, rev, finalize):
    if finalize:
        oprev_ref, gate_ref, nw_ref, o_ref, s_ref = rest
    else:
        o_ref, s_ref = rest

    @pl.when(pl.program_id(1) == 0)
    def _():
        s_ref[...] = jnp.zeros_like(s_ref)

    m_incl, m_strict, m_blk, m_mid, m_top, eye = (m_ref[n] for n in range(6))
    order = list(_chunk_order(rev))
    units = [(c, h) for c in order for h in range(N_HEAD)]
    rows = lambda c: slice(c * CHUNK, (c + 1) * CHUNK)
    cols = lambda h: slice(h * GDN_DK, (h + 1) * GDN_DK)
    d = 1 if rev else 0
    lane_a = lambda h: slice(SM_A + d * N_HEAD + h, SM_A + d * N_HEAD + h + 1)
    lane_b = lambda h: slice(SM_B + d * N_HEAD + h, SM_B + d * N_HEAD + h + 1)

    x = sm_ref[...]
    xa = x + dtb_ref[...]
    g_all = -ea_ref[...] * (jnp.maximum(xa, 0.0) + jnp.log(1.0 + jnp.exp(-jnp.abs(xa))))
    beta_all = jax.nn.sigmoid(x)
    incl16 = m_incl.astype(BF16)
    G = {c: _dot_const(incl16, g_all[rows(c)], 3) for c in order}
    G_rows = {c: sum(_dot_nt(sel_ref[...], p_) for p_ in _split_bf16(G[c], 3)) for c in order}

    k16 = [k_ref[rows(c), cols(h)] for c, h in units]
    q = [q_ref[rows(c), cols(h)].astype(F32) for c, h in units]
    k = [t.astype(F32) for t in k16]
    v = [v_ref[rows(c), cols(h)].astype(F32) for c, h in units]
    g_col = [G[c][:, lane_a(h)] for c, h in units]
    beta = [beta_all[rows(c), lane_b(h)] for c, h in units]
    g_row = [G_rows[c][h:h + 1, :] for c, h in units]
    g_tot = [g[0:1] if rev else g[CHUNK - 1:CHUNK] for g in g_col]
    decay = [jnp.exp(jnp.where(m_incl > 0, gc - gr, -jnp.inf)) for gc, gr in zip(g_col, g_row)]
    k_beta = [t * b for t, b in zip(k, beta)]
    a = [-(_dot_nt(kb.astype(BF16), kk) * d) * m_strict for kb, kk, d in zip(k_beta, k16, decay)]
    t16 = [(x - eye).astype(BF16) for x in _unit_tri_inverse(a, m_blk, m_mid, m_top, eye)]
    v_beta = [t * b for t, b in zip(v, beta)]
    e_col = [jnp.exp(g) for g in g_col]
    kbg = [kb * e for kb, e in zip(k_beta, e_col)]
    u = [vb + _dot(t, vb.astype(BF16)) for t, vb in zip(t16, v_beta)]
    w16 = [(x + _dot(t, x.astype(BF16))).astype(BF16) for t, x in zip(t16, kbg)]
    sc16 = [(_dot_nt(qq.astype(BF16), kk) * d).astype(BF16) for qq, kk, d in zip(q, k16, decay)]
    q_in = [(qq * e).astype(BF16) for qq, e in zip(q, e_col)]
    k_out = [(kk * jnp.exp(gt - gc)).astype(BF16) for kk, gt, gc in zip(k, g_tot, g_col)]
    a_end = [jnp.exp(gt) for gt in g_tot]

    s = [s_ref[h] for h in range(N_HEAD)]
    for ci in range(CPB):
        us = range(ci * N_HEAD, (ci + 1) * N_HEAD)
        s16 = [t.astype(BF16) for t in s]
        vn16 = [(u[n] - _dot(w16[n], s16[h])).astype(BF16) for h, n in enumerate(us)]
        o = [_dot(q_in[n], s16[h]) + _dot(sc16[n], vn16[h]) for h, n in enumerate(us)]
        s = [a_end[n] * s[h] + _dot_tn(k_out[n], vn16[h]) for h, n in enumerate(us)]
        for h, n in enumerate(us):
            c = units[n][0]
            oh = o[h]
            if finalize:
                oh = _finalize(oh, oprev_ref, gate_ref, nw_ref, c * CHUNK, h * GDN_DV, GDN_DV)
            o_ref[rows(c), cols(h)] = oh.astype(o_ref.dtype)
    for h in range(N_HEAD):
        s_ref[h] = s[h]


PREP_SLAB = 512


def _conv_prep_kernel(z_ref, w_ref, b_ref, o_ref, *, gdn):
    L = z_ref.shape[0]
    slab = min(PREP_SLAB, L)
    cb = pl.program_id(1)
    w0, w1, w2 = w_ref[0:1, :], w_ref[1:2, :], w_ref[2:3, :]
    row = lax.broadcasted_iota(jnp.int32, (slab, z_ref.shape[1]), 0)
    zero_row = jnp.zeros((1, z_ref.shape[1]), F32)
    for r0 in range(0, L, slab):
        z = z_ref[r0:r0 + slab, :].astype(F32)
        halo = 16
        before = z_ref[r0 - halo:r0, :].astype(F32)[halo - 1:halo] if r0 > 0 else zero_row
        after = z_ref[r0 + slab:r0 + slab + halo, :].astype(F32)[0:1] if r0 + slab < L else zero_row
        prev = jnp.where(row == 0, before, pltpu.roll(z, 1, 0))
        nxt = jnp.where(row == slab - 1, after, pltpu.roll(z, slab - 1, 0))
        y = prev * w0 + z * w1 + nxt * w2
        if not gdn:
            o_ref[r0:r0 + slab, :] = (y + b_ref[...]).astype(o_ref.dtype)
            continue
        y = _silu(y)

        @pl.when(cb < 2)
        def _():
            scale = jnp.where(cb == 0, GDN_DK ** -0.5, 1.0)
            for h in range(N_HEAD):
                t = y[:, h * GDN_DK:(h + 1) * GDN_DK]
                t = t * (lax.rsqrt(jnp.sum(t * t, axis=-1, keepdims=True) + EPS) * scale)
                o_ref[r0:r0 + slab, h * GDN_DK:(h + 1) * GDN_DK] = t.astype(o_ref.dtype)

        @pl.when(cb == 2)
        def _():
            o_ref[r0:r0 + slab, :] = y.astype(o_ref.dtype)


def _conv_prep(p, col0, conv_w, conv_b, n_seq, seq_len, gdn):
    T = p.shape[0]
    dtype = BF16 if gdn else F32
    bias = jnp.zeros((1, 3 * COLW), F32) if conv_b is None else conv_b.reshape(1, -1)
    outs = []
    for L, row0 in ((seq_len, 0), (CTX_LEN, n_seq * seq_len // CTX_LEN)):
        outs.append(pl.pallas_call(
            functools.partial(_conv_prep_kernel, gdn=gdn),
            grid=(n_seq, 3),
            in_specs=[pl.BlockSpec((L, COLW), lambda b, c, row0=row0: (row0 + b, col0 + c)),
                      pl.BlockSpec((3, COLW), lambda b, c: (0, c)),
                      pl.BlockSpec((1, COLW), lambda b, c: (0, c))],
            out_specs=pl.BlockSpec((L, COLW), lambda b, c: (b, c)),
            out_shape=jax.ShapeDtypeStruct((n_seq * L, 3 * COLW), dtype),
            compiler_params=_cparams(("parallel", "parallel")),
            name="conv_prep_%s_L%d" % ("gdn" if gdn else "hyena", L),
        )(p, conv_w, bias))
    return outs


def _gdn(p, p_small, conv_w, a_log, dt_bias, norm_w, n_seq, n_lat_blk):
    T = p.shape[0]
    qkv = jnp.concatenate(_conv_prep(p, CB_GDN_Q, conv_w, None, n_seq, n_lat_blk * SCAN_BLK, True),
                          axis=0)
    pad = lambda t: jnp.pad(t.reshape(1, -1), ((0, 0), (SM_A, V7X_LANES - SM_A - 2 * N_HEAD)))
    exp_a, dtb = pad(jnp.exp(a_log)), pad(dt_bias)
    out = None
    for rev in (True, False):
        finalize = not rev
        d = 1 if rev else 0
        sel = np.zeros((16, V7X_LANES), np.float32)
        sel[np.arange(N_HEAD), SM_A + d * N_HEAD + np.arange(N_HEAD)] = 1.0
        rows = _scan_rows(n_seq, n_lat_blk, rev)
        extra = [(p_small, pl.BlockSpec((SCAN_BLK, V7X_LANES), lambda b_, j, rows=rows: (rows(b_, j), 0)))]
        kern = functools.partial(_gdn_kernel, rev=rev, finalize=finalize)
        out = _scan_call(kern, "gdn_bwd" if rev else "gdn_fwd", n_seq, n_lat_blk, rev, finalize,
                         [(qkv, 0), (qkv, 1), (qkv, 2)], [],
                         [exp_a, dtb, jnp.asarray(sel, BF16), _gdn_consts(rev)], extra,
                         out, (p, CB_GDN_GATE), norm_w,
                         [pltpu.VMEM((N_HEAD, GDN_DK, GDN_DV), F32)])
    return out


GLA_SUB = 4


def _gla_consts(rev):
    C = CHUNK
    idx = np.arange(C)
    before = _before(rev)
    incl = before | np.eye(C, dtype=bool)
    sizes = (16, 4, 1)
    blk = [idx // s for s in sizes]
    par = [idx // 64, idx // 16, idx // 4]
    sub = [b % GLA_SUB for b in blk]

    def blk_before(l, a, b):
        return (blk[l][a] > blk[l][b]) if rev else (blk[l][a] < blk[l][b])

    pq = np.zeros((2, C, C), np.float32)
    for l in range(2):
        pq[l] = incl & (blk[l][:, None] == blk[l][None, :])
    kmt = np.zeros((3, C, GLA_SUB * C), np.float32)
    maskx = np.zeros((4, C, GLA_SUB * C), np.float32)
    for l in range(3):
        for m in range(GLA_SUB):
            for j in range(C):
                tgt = par[l][j] * GLA_SUB + m
                valid = (tgt < blk[l][j]) if rev else (tgt > blk[l][j])
                if not valid:
                    continue
                t_after_j = before[:, j]
                t_blk_before_tgt = (blk[l] > tgt) if rev else (blk[l] < tgt)
                kmt[l, :, m * C + j] = t_after_j & t_blk_before_tgt
            for i in range(C):
                if sub[l][i] != m:
                    continue
                sel = (par[l] == par[l][i]) & np.array([blk_before(l, j, i) for j in range(C)])
                maskx[l, i, m * C:(m + 1) * C] = sel
    maskx[3, :, :C] = np.eye(C)
    id4 = np.tile(np.eye(C, dtype=np.float32), (1, GLA_SUB))
    return (jnp.asarray(incl.astype(np.float32), BF16), jnp.asarray(pq, BF16), jnp.asarray(kmt, BF16),
            jnp.asarray(maskx, F32), jnp.asarray(id4, BF16))


def _gla_kernel(qk_ref, v_ref, sm_ref, w2_ref, b2_ref, incl_ref, pq_ref, kmt_ref, mx_ref, id4_ref, *rest,
                rev, finalize):
    if finalize:
        oprev_ref, gate_ref, nw_ref, o_ref, st_ref = rest
    else:
        o_ref, st_ref = rest

    @pl.when(pl.program_id(1) == 0)
    def _():
        st_ref[...] = jnp.zeros_like(st_ref)

    hw = N_HEAD * GLA_DK
    x = _dot(sm_ref[...].astype(BF16), w2_ref[...]) + b2_ref[...]
    gk_all = (jnp.minimum(x, 0.0) - jnp.log(1.0 + jnp.exp(-jnp.abs(x)))) * (1.0 / GLA_TAU)
    incl = incl_ref[...]
    id4 = id4_ref[...]
    order = list(_chunk_order(rev))
    units = [(ci, h) for ci in range(CPB) for h in range(N_HEAD)]
    rows = lambda c: slice(c * CHUNK, (c + 1) * CHUNK)
    kcols = lambda h: slice(h * GLA_DK, (h + 1) * GLA_DK)
    vcols = lambda h: slice(h * GLA_DV, (h + 1) * GLA_DV)

    gk = [gk_all[rows(c)] for c in order]
    q = [qk_ref[rows(c), :hw].astype(F32) * GLA_DK ** -0.5 for c in order]
    k = [qk_ref[rows(c), hw:].astype(F32) for c in order]
    vb = [v_ref[rows(c), :].astype(BF16) for c in order]
    G = [_dot_const(incl, g, 3) for g in gk]
    g_tot = [g[0:1] if rev else g[CHUNK - 1:CHUNK] for g in G]
    e1 = [_dot_const(pq_ref[0], g, 3) for g in gk]
    e2 = [_dot_const(pq_ref[1], g, 3) for g in gk]
    q_lvl = [[(qq * jnp.exp(e)).astype(BF16) for qq, e in zip(q, es)] for es in (e1, e2, gk)]
    q_lvl.append([qq.astype(BF16) for qq in q])
    q_in = [(qq * jnp.exp(g)).astype(BF16) for qq, g in zip(q, G)]
    k_out = [(kk * jnp.exp(gt - g)).astype(BF16) for kk, gt, g in zip(k, g_tot, G)]
    kb = [kk.astype(BF16) for kk in k]
    decay_tot = [jnp.exp(gt) for gt in g_tot]

    kt_all = [_dot_tn(t, id4) for t in kb]
    kt = [kt_all[ci][kcols(h)] for ci, h in units]
    sx = [_dot(q_lvl[3][ci][:, kcols(h)], t.astype(BF16)) * mx_ref[3] for (ci, h), t in zip(units, kt)]
    for l in range(3):
        e_all = [_dot_tn_const(g, kmt_ref[l], 2) for g in gk]
        e = [e_all[ci][kcols(h)] for ci, h in units]
        kt_l = [(t * jnp.exp(e_)).astype(BF16) for t, e_ in zip(kt, e)]
        sx = [s_ + _dot(q_lvl[l][ci][:, kcols(h)], t) * mx_ref[l] for (ci, h), s_, t in zip(units, sx, kt_l)]
    vh = [vb[ci][:, vcols(h)] for ci, h in units]
    o_intra = [_dot(s_.astype(BF16), jnp.concatenate([v_] * GLA_SUB, axis=0)) for s_, v_ in zip(sx, vh)]
    kv = [_dot_tn(v_, k_out[ci][:, kcols(h)]) for (ci, h), v_ in zip(units, vh)]

    st = [st_ref[h] for h in range(N_HEAD)]
    for ci in range(CPB):
        c = order[ci]
        for h in range(N_HEAD):
            n = ci * N_HEAD + h
            o = o_intra[n] + _dot_nt(q_in[ci][:, kcols(h)], st[h].astype(BF16))
            st[h] = st[h] * decay_tot[ci][:, kcols(h)] + kv[n]
            if finalize:
                o = _finalize(o, oprev_ref, gate_ref, nw_ref, c * CHUNK, h * GLA_DV, GLA_DV)
            o_ref[rows(c), vcols(h)] = o.astype(o_ref.dtype)
    for h in range(N_HEAD):
        st_ref[h] = st[h]


def _gla(p, p_small, w2, b2, norm_w, n_seq, n_lat_blk):
    out = None
    hw = N_HEAD * GLA_DK
    for rev in (True, False):
        finalize = not rev
        d = 1 if rev else 0
        lr0 = SM_LR + d * GLA_RANK
        w2p = jnp.zeros((V7X_LANES, hw), F32).at[lr0:lr0 + GLA_RANK].set(w2[d]).astype(BF16)
        rows = _scan_rows(n_seq, n_lat_blk, rev)
        extra = [(p_small, pl.BlockSpec((SCAN_BLK, V7X_LANES), lambda b_, j, rows=rows: (rows(b_, j), 0)))]
        kern = functools.partial(_gla_kernel, rev=rev, finalize=finalize)
        out = _scan_call(kern, "gla_bwd" if rev else "gla_fwd", n_seq, n_lat_blk, rev, finalize,
                         [(p, CB_GLA_QK), (p, CB_GLA_V)], [], [w2p, b2[d].reshape(1, hw), *_gla_consts(rev)],
                         extra, out, (p, CB_GLA_GATE), norm_w,
                         [pltpu.VMEM((N_HEAD, GLA_DV, GLA_DK), F32)])
    return out


def _hyena_filters(L, w1, b1, w2, b2, w3, freq):
    t = jnp.linspace(0.0, 1.0, L, dtype=F32)[:, None]
    w = 2.0 * math.pi * jnp.arange(L, dtype=F32) / L
    f = jnp.linspace(1e-4, HY_BANDS - 1, HY_BANDS, dtype=F32)
    ang = w[:, None] * f[None, :]
    z = jnp.concatenate([t, jnp.cos(ang), -jnp.sin(ang)], axis=-1)
    hdn = jnp.sin(freq[0] * (z @ w1 + b1))
    for i in range(HY_INNER):
        hdn = jnp.sin(freq[i + 1] * (hdn @ w2[i] + b2[i]))
    h = (hdn @ w3).reshape(L, HY_ORDER, 2, HY_W)
    deltas = jnp.abs(jnp.linspace(HY_MIN_DECAY, HY_MAX_DECAY, HY_W, dtype=F32))
    h = h * jnp.exp(-t[:, :, None, None] * deltas)
    hf, hb = h[:, :, 0], h[:, :, 1]
    hb = hb.at[0].set(0.0)
    scale = lax.rsqrt(jnp.sum(hf * hf, axis=0) + jnp.sum(hb * hb, axis=0) + EPS)
    return (hf * scale).reshape(L, -1), (hb * scale).reshape(L, -1)


def _dft_mats(L):
    k = jnp.arange(L, dtype=jnp.int32)[:, None]
    n = jnp.arange(L, dtype=jnp.int32)[None, :]
    ang = (((2 * k + 1) * n) % (4 * L)).astype(F32) * (math.pi / (2 * L))
    return jnp.cos(ang).astype(BF16), jnp.sin(ang).astype(BF16)


def _hy_spectra(hf, hb, dft):
    def mm2(m, g):
        n = g.shape[1]
        out = _matmul(m, jnp.concatenate(_split_bf16(g, 2), axis=1))
        return out[:, :n] + out[:, n:]

    return mm2(dft[0], hf + hb), mm2(dft[1], hb - hf)


def _hy_kernel(z_ref, x_ref, skip_ref, c_ref, s_ref, kr_ref, ki_ref, *rest, last, n_k, slab, inv_scale):
    if last:
        gate_ref, o_ref, zb_ref, acc_ref = rest
    else:
        o_ref, zb_ref, acc_ref = rest
    kt = pl.program_id(2)

    @pl.when(kt == 0)
    def _():
        zb_ref[...] = z_ref[...].astype(BF16)
        acc_ref[...] = jnp.zeros_like(acc_ref)

    c, s = c_ref[...], s_ref[...]
    zb = zb_ref[...]
    zr, zi = _dot(c, zb), -_dot(s, zb)
    kr, ki = kr_ref[...], ki_ref[...]
    pr = (zr * kr - zi * ki).T.astype(BF16)
    pi = (zr * ki + zi * kr).T.astype(BF16)
    acc_ref[...] += _dot(pr, c) - _dot(pi, s)

    @pl.when(kt == n_k - 1)
    def _():
        for r0 in range(0, acc_ref.shape[1], slab):
            conv = acc_ref[:, r0:r0 + slab].T * inv_scale
            z = z_ref[r0:r0 + slab, :]
            y = x_ref[r0:r0 + slab, :] * (conv + skip_ref[...] * z)
            if last:
                y = y * _silu(gate_ref[r0:r0 + slab, :].astype(F32))
            o_ref[r0:r0 + slab, :] = y.astype(o_ref.dtype)


HY_WT = 256


def _hy_order(z_src, x_src, gate_src, skip, dft, kr, ki, order, n_seq, L):
    last = gate_src is not None
    tk = min(512, L)
    n_k = L // tk
    slab = min(512, L)
    n_wt = HY_W // HY_WT

    def rowspec(src):
        _, r0, c0 = src
        return pl.BlockSpec((L, HY_WT), lambda b, w, k: (r0 + b, c0 + w), pipeline_mode=pl.Buffered(1))

    srcs = [z_src, x_src] + ([gate_src] if last else [])
    twid = pl.BlockSpec((tk, L), lambda b, w, k: (k, 0))
    spec = pl.BlockSpec((tk, HY_WT), lambda b, w, k: (k, order * n_wt + w))
    kern = functools.partial(_hy_kernel, last=last, n_k=n_k, slab=slab, inv_scale=1.0 / L)
    args = [z_src[0], x_src[0], skip[order].reshape(1, HY_W), dft[0], dft[1], kr, ki]
    specs = [rowspec(z_src), rowspec(x_src), pl.BlockSpec((1, HY_WT), lambda b, w, k: (0, w)),
             twid, twid, spec, spec]
    if last:
        args.append(gate_src[0])
        specs.append(rowspec(gate_src))
    return pl.pallas_call(
        kern,
        grid=(n_seq, n_wt, n_k),
        in_specs=specs,
        out_specs=pl.BlockSpec((L, HY_WT), lambda b, w, k: (b, w)),
        out_shape=jax.ShapeDtypeStruct((n_seq * L, HY_W), BF16 if last else F32),
        scratch_shapes=[pltpu.VMEM((L, HY_WT), BF16), pltpu.VMEM((HY_WT, L), F32)],
        compiler_params=_cparams(("parallel", "parallel", "arbitrary")),
        name="hyena_order%d_L%d" % (order, L),
    )(*args)


def _hyena(p, conv_w, conv_b, w1, b1, w2, b2, w3, freq, skip, dfts, n_seq, seq_len):
    zcs = _conv_prep(p, CB_HY_V, conv_w, conv_b, n_seq, seq_len, False)
    n_wt = HY_W // HY_WT
    outs = []
    for zc, L, row0 in zip(zcs, (seq_len, CTX_LEN), (0, n_seq * seq_len // CTX_LEN)):
        dft = dfts[L]
        kr, ki = _hy_spectra(*_hyena_filters(L, w1, b1, w2, b2, w3, freq), dft)
        y1 = _hy_order((zc, 0, 0), (zc, 0, n_wt), None, skip, dft, kr, ki, 0, n_seq, L)
        outs.append(_hy_order((y1, 0, 0), (zc, 0, 2 * n_wt), (p, row0, CB_HY_GATE * n_wt),
                              skip, dft, kr, ki, 1, n_seq, L))
    return jnp.concatenate(outs, axis=0)


def _layout_w_in(w_in):
    sizes = [3 * HY_W, HY_W, 3 * BR_W, 2 * N_HEAD, 2 * N_HEAD, BR_W, 2 * N_HEAD * RET_DK, BR_W, BR_W,
             2 * N_HEAD * GLA_DK, BR_W, 2 * GLA_RANK, BR_W]
    offs = np.concatenate([[0], np.cumsum(sizes)])
    (hy_proj, hy_gate, gdn_qkv, gdn_a, gdn_b, gdn_gate, ret_qk, ret_v, ret_gate,
     gla_qk, gla_v, gla_lr, gla_gate) = [w_in[:, offs[n]:offs[n + 1]] for n in range(len(sizes))]
    D = w_in.shape[0]
    half = RET_DK // 2
    ret_rot = ret_qk.reshape(D, 2 * N_HEAD, 2, half)[:, :, ::-1].reshape(D, -1)
    main = jnp.concatenate([hy_proj, hy_gate, gdn_qkv, gdn_gate, ret_qk, ret_rot, ret_v, ret_gate,
                            gla_qk, gla_v, gla_gate], axis=1)
    small = jnp.concatenate([gdn_a, gdn_b, gla_lr], axis=1)
    small = jnp.pad(small, ((0, 0), (0, V7X_LANES - small.shape[1])))
    return main, small


def _layer(x, mod, dims, rope, dfts, norm_w, w_in, hy_conv_w, hy_conv_b, hy_w1, hy_b1, hy_w2, hy_b2, hy_w3,
           hy_freq, hy_skip, gdn_conv_w, gdn_a_log, gdn_dt_bias, gdn_norm, ret_norm, gla_w2, gla_b2,
           gla_norm, w_branch, w_merge, b_merge, w_out):
    n_seq, seq_len = dims
    n_lat_rows = n_seq * seq_len
    n_lat_blk = seq_len // SCAN_BLK
    shift, scale, gate = mod
    w_main, w_small = _layout_w_in(w_in.astype(BF16))
    p, h = _inproj(x, norm_w, scale, shift, w_main, n_lat_rows, seq_len)
    p_small = _matmul(h, w_small)

    y_hy = _hyena(p, hy_conv_w, hy_conv_b, hy_w1, hy_b1, hy_w2, hy_b2, hy_w3, hy_freq, hy_skip, dfts,
                  n_seq, seq_len)
    y_gdn = _gdn(p, p_small, gdn_conv_w, gdn_a_log, gdn_dt_bias, gdn_norm, n_seq, n_lat_blk)
    y_ret = _retention(p, rope[0], rope[1], ret_norm, n_seq, n_lat_blk)
    y_gla = _gla(p, p_small, gla_w2, gla_b2, gla_norm, n_seq, n_lat_blk)

    merged = _merge(h, (y_hy, y_gdn, y_ret, y_gla), w_merge.astype(BF16), b_merge, w_branch.astype(BF16))
    return _outproj(merged, w_out.astype(BF16), x, gate, n_lat_rows, seq_len)


def kernel(x, c, ctx, c_ctx, norm_w, ada_w, ada_b, w_in, hy_conv_w, hy_conv_b, hy_w1, hy_b1,
           hy_w2, hy_b2, hy_w3, hy_freq, hy_skip, gdn_conv_w, gdn_a_log, gdn_dt_bias,
           gdn_norm, ret_norm, gla_w2, gla_b2, gla_norm, w_branch, w_merge, b_merge, w_out,
           final_norm):
    B, S, D = x.shape
    layer_params = (norm_w, w_in, hy_conv_w, hy_conv_b, hy_w1, hy_b1, hy_w2, hy_b2, hy_w3, hy_freq,
                    hy_skip, gdn_conv_w, gdn_a_log, gdn_dt_bias, gdn_norm, ret_norm, gla_w2, gla_b2,
                    gla_norm, w_branch, w_merge, b_merge, w_out)
    xt = jnp.concatenate([x.reshape(-1, D), ctx.reshape(-1, D)], axis=0)
    cond = _silu(jnp.concatenate([c_ctx[None], c], axis=0))
    cond = jnp.pad(cond, ((0, -(1 + B) % 8), (0, 0))).astype(BF16)
    rope = _rope_tables(S)
    dfts = {L: _dft_mats(L) for L in (S, CTX_LEN)}
    for l in range(DEPTH):
        m = (_matmul(cond, ada_w[l].astype(BF16)) + ada_b[l])[:1 + B]
        mod = tuple(t.reshape(1 + B, 1, D) for t in jnp.split(m, 3, axis=-1))
        xt = _layer(xt, mod, (B, S), rope, dfts, *[p[l] for p in layer_params])
    return _final_norm(xt, final_norm, B * S).reshape(B, S, D)
```

```python
import functools
import math

import numpy as np
import jax
import jax.numpy as jnp
from jax import lax
from jax.experimental import pallas as pl
from jax.experimental.pallas import tpu as pltpu

D_MODEL = 2048
DEPTH = 4
CTX_LEN = 256
GRID_W = 64
F32 = jnp.float32
BF16 = jnp.bfloat16
EPS = 1e-6
CHUNK = 64

N_BRANCH = 4
BR_W = D_MODEL // N_BRANCH

HY_W = BR_W
HY_ORDER = 2
HY_EMB = 33
HY_BANDS = (HY_EMB - 1) // 2
HY_INNER = 2
HY_FAST_DECAY = 0.3
HY_SLOW_DECAY = 1.5
HY_TARGET = 1e-2
HY_MIN_DECAY = math.log(HY_TARGET) / HY_SLOW_DECAY
HY_MAX_DECAY = math.log(HY_TARGET) / HY_FAST_DECAY

N_HEAD = 4
GDN_DK = BR_W // N_HEAD
GDN_DV = BR_W // N_HEAD
RET_DK = BR_W // (2 * N_HEAD)
RET_DV = BR_W // N_HEAD
RET_ROPE_PAIRS = (8, 12, 12)
ROPE_BASE = 10000.0
GLA_DK = BR_W // (2 * N_HEAD)
GLA_DV = BR_W // N_HEAD
GLA_RANK = 16
GLA_TAU = 16.0

V7X_LANES = 128
VMEM_LIMIT_BYTES = 56 * 1024 * 1024

SCAN_BLK = CTX_LEN
CPB = SCAN_BLK // CHUNK
COLW = BR_W

CB_HY_V, CB_HY_X1, CB_HY_X2, CB_HY_GATE = 0, 1, 2, 3
CB_GDN_Q, CB_GDN_GATE = 4, 7
CB_RET_QK, CB_RET_QKROT, CB_RET_V, CB_RET_GATE = 8, 9, 10, 11
CB_GLA_QK, CB_GLA_V, CB_GLA_GATE = 12, 13, 14
N_COLBLK = 15
SM_A, SM_B, SM_LR = 0, 2 * N_HEAD, 4 * N_HEAD


def _dot(a, b):
    return jnp.dot(a, b, preferred_element_type=F32)


def _dot_nt(a, b):
    return lax.dot_general(a, b, (((1,), (1,)), ((), ())), preferred_element_type=F32)


def _dot_tn(a, b):
    return lax.dot_general(a, b, (((0,), (0,)), ((), ())), preferred_element_type=F32)


def _split_bf16(x, terms):
    parts = []
    r = x
    for t in range(terms):
        p = r.astype(BF16)
        parts.append(p)
        if t + 1 < terms:
            r = r - p.astype(F32)
    return parts


def _dot_f32(a, b):
    a1, a2 = _split_bf16(a, 2)
    b1, b2 = _split_bf16(b, 2)
    return _dot(a1, b1) + (_dot(a1, b2) + _dot(a2, b1))


def _dot_const(c, x, terms):
    out = None
    for p in _split_bf16(x, terms):
        t = _dot(c, p)
        out = t if out is None else out + t
    return out


def _dot_tn_const(x, c, terms):
    out = None
    for p in _split_bf16(x, terms):
        t = _dot_tn(p, c)
        out = t if out is None else out + t
    return out


def _silu(x):
    return x * jax.nn.sigmoid(x)


def _cparams(sem):
    return pltpu.CompilerParams(dimension_semantics=sem, vmem_limit_bytes=VMEM_LIMIT_BYTES)


def _mm_kernel(a_ref, b_ref, o_ref):
    o_ref[...] = _dot(a_ref[...], b_ref[...]).astype(o_ref.dtype)


def _pick_tile(n, cands):
    for c in cands:
        if n % c == 0:
            return c
    return n


def _matmul(a, b, out_dtype=F32):
    M, K = a.shape
    _, N = b.shape
    tm = _pick_tile(M, (1024, 512, 256, 128, 64, 32, 16, 8))
    tn = _pick_tile(N, (512, 384, 256, 128))
    return pl.pallas_call(
        _mm_kernel,
        grid=(M // tm, N // tn),
        in_specs=[pl.BlockSpec((tm, K), lambda i, j: (i, 0)),
                  pl.BlockSpec((K, tn), lambda i, j: (0, j))],
        out_specs=pl.BlockSpec((tm, tn), lambda i, j: (i, j)),
        out_shape=jax.ShapeDtypeStruct((M, N), out_dtype),
        compiler_params=_cparams(("parallel", "arbitrary")),
        name="matmul",
    )(a, b)


def _mod_row(n_lat_blk, blk_per_seq):
    return lambda i: jnp.where(i < n_lat_blk, 1 + i // blk_per_seq, 0)


def _inproj_kernel(x_ref, nw_ref, sc_ref, sh_ref, w_ref, p_ref, h_ref, hs_ref):
    @pl.when(pl.program_id(1) == 0)
    def _():
        for r0 in range(0, x_ref.shape[0], PREP_SLAB):
            x = x_ref[r0:r0 + PREP_SLAB, :]
            y = x * lax.rsqrt(jnp.mean(x * x, axis=-1, keepdims=True) + EPS) * nw_ref[...]
            h = (y * (1.0 + sc_ref[0]) + sh_ref[0]).astype(BF16)
            hs_ref[r0:r0 + PREP_SLAB, :] = h
            h_ref[r0:r0 + PREP_SLAB, :] = h

    p_ref[...] = _dot(hs_ref[...], w_ref[...]).astype(p_ref.dtype)


def _inproj(x, norm_w, scale, shift, w, n_lat_rows, seq_len):
    T, D = x.shape
    N = w.shape[1]
    tm, tn = 1024, 3 * COLW
    row = _mod_row(n_lat_rows // tm, seq_len // tm)
    return pl.pallas_call(
        _inproj_kernel,
        grid=(T // tm, N // tn),
        in_specs=[pl.BlockSpec((tm, D), lambda i, j: (i, 0)),
                  pl.BlockSpec((1, D), lambda i, j: (0, 0)),
                  pl.BlockSpec((1, 1, D), lambda i, j: (row(i), 0, 0)),
                  pl.BlockSpec((1, 1, D), lambda i, j: (row(i), 0, 0)),
                  pl.BlockSpec((D, tn), lambda i, j: (0, j))],
        out_specs=[pl.BlockSpec((tm, tn), lambda i, j: (i, j)),
                   pl.BlockSpec((tm, D), lambda i, j: (i, 0))],
        out_shape=[jax.ShapeDtypeStruct((T, N), BF16), jax.ShapeDtypeStruct((T, D), BF16)],
        scratch_shapes=[pltpu.VMEM((tm, D), BF16)],
        compiler_params=_cparams(("parallel", "arbitrary")),
        name="inproj",
    )(x, norm_w.reshape(1, D), scale, shift, w)


def _merge_kernel(h_ref, y0_ref, y1_ref, y2_ref, y3_ref, wm_ref, bm_ref, wb_ref, o_ref):
    h = h_ref[...]
    acc = None
    for i, y_ref in enumerate((y0_ref, y1_ref, y2_ref, y3_ref)):
        g = jax.nn.sigmoid(_dot(h, wm_ref[i]) + bm_ref[i])
        t = g * _dot(y_ref[...], wb_ref[i])
        acc = t if acc is None else acc + t
    o_ref[...] = acc.astype(o_ref.dtype)


def _merge(h, ys, w_merge, b_merge, w_branch):
    T, D = h.shape
    tm, tn = 1024, 512
    yspec = pl.BlockSpec((tm, BR_W), lambda i, j: (i, 0))
    return pl.pallas_call(
        _merge_kernel,
        grid=(T // tm, D // tn),
        in_specs=[pl.BlockSpec((tm, D), lambda i, j: (i, 0)), yspec, yspec, yspec, yspec,
                  pl.BlockSpec((N_BRANCH, D, tn), lambda i, j: (0, 0, j)),
                  pl.BlockSpec((N_BRANCH, 1, tn), lambda i, j: (0, 0, j)),
                  pl.BlockSpec((N_BRANCH, BR_W, tn), lambda i, j: (0, 0, j))],
        out_specs=pl.BlockSpec((tm, tn), lambda i, j: (i, j)),
        out_shape=jax.ShapeDtypeStruct((T, D), BF16),
        compiler_params=_cparams(("parallel", "arbitrary")),
        name="merge",
    )(h, *ys, w_merge, b_merge.reshape(N_BRANCH, 1, D), w_branch)


def _outproj_kernel(a_ref, w_ref, x_ref, g_ref, o_ref):
    o_ref[...] = x_ref[...] + g_ref[0] * _dot(a_ref[...], w_ref[...])


def _outproj(a, w, x, gate, n_lat_rows, seq_len):
    T, D = x.shape
    tm, tn = 1024, 1024
    row = _mod_row(n_lat_rows // tm, seq_len // tm)
    return pl.pallas_call(
        _outproj_kernel,
        grid=(T // tm, D // tn),
        in_specs=[pl.BlockSpec((tm, D), lambda i, j: (i, 0)),
                  pl.BlockSpec((D, tn), lambda i, j: (0, j)),
                  pl.BlockSpec((tm, tn), lambda i, j: (i, j)),
                  pl.BlockSpec((1, 1, tn), lambda i, j: (row(i), 0, j))],
        out_specs=pl.BlockSpec((tm, tn), lambda i, j: (i, j)),
        out_shape=jax.ShapeDtypeStruct((T, D), F32),
        compiler_params=_cparams(("parallel", "arbitrary")),
        name="outproj",
    )(a, w, x, gate)


def _final_norm_kernel(x_ref, w_ref, o_ref):
    x = x_ref[...]
    o_ref[...] = x * lax.rsqrt(jnp.mean(x * x, axis=-1, keepdims=True) + EPS) * w_ref[...]


def _final_norm(x, w, n_lat_rows):
    D = x.shape[1]
    tm = 512
    return pl.pallas_call(
        _final_norm_kernel,
        grid=(n_lat_rows // tm,),
        in_specs=[pl.BlockSpec((tm, D), lambda i: (i, 0)),
                  pl.BlockSpec((1, D), lambda i: (0, 0))],
        out_specs=pl.BlockSpec((tm, D), lambda i: (i, 0)),
        out_shape=jax.ShapeDtypeStruct((n_lat_rows, D), F32),
        compiler_params=_cparams(("parallel",)),
        name="final_norm",
    )(x, w.reshape(1, D))


def _scan_rows(n_seq, n_lat_blk, rev):
    def idx(b, j):
        jj = (n_lat_blk - j) if rev else (j - 1)
        return jnp.where(j == 0, n_seq * n_lat_blk + b, b * n_lat_blk + jj)
    return idx


def _chunk_order(rev):
    return range(CPB - 1, -1, -1) if rev else range(CPB)


def _before(rev):
    i = np.arange(CHUNK)[:, None]
    j = np.arange(CHUNK)[None, :]
    return (j > i) if rev else (j < i)


def _finalize(o, oprev_ref, gate_ref, nw_ref, r0, c0, width):
    o = o + oprev_ref[r0:r0 + CHUNK, c0:c0 + width]
    o = o * lax.rsqrt(jnp.mean(o * o, axis=-1, keepdims=True) + EPS) * nw_ref[...]
    return o * _silu(gate_ref[r0:r0 + CHUNK, c0:c0 + width].astype(F32))


def _scan_call(kernel, name, n_seq, n_lat_blk, rev, finalize, row_inputs, tab_inputs, const_inputs,
               extra_specs_inputs, oprev, gate_src, norm_w, scratch):
    rows = _scan_rows(n_seq, n_lat_blk, rev)
    T = row_inputs[0][0].shape[0]
    args, specs = [], []
    for arr, cb in row_inputs:
        args.append(arr)
        specs.append(pl.BlockSpec((SCAN_BLK, COLW), functools.partial(
            lambda b, j, cb: (rows(b, j), cb), cb=cb)))
    for arr in tab_inputs:
        args.append(arr)
        specs.append(pl.BlockSpec((SCAN_BLK, arr.shape[1]), lambda b, j: (
            jnp.where(j == 0, 0, 1 + ((n_lat_blk - j) if rev else (j - 1))), 0)))
    for arr, spec in extra_specs_inputs:
        args.append(arr)
        specs.append(spec)
    for arr in const_inputs:
        args.append(arr)
        specs.append(pl.BlockSpec(arr.shape, functools.partial(lambda b, j, n: (0,) * n, n=arr.ndim)))
    if finalize:
        args += [oprev, gate_src[0], norm_w.reshape(1, -1)]
        specs += [pl.BlockSpec((SCAN_BLK, COLW), lambda b, j: (rows(b, j), 0)),
                  pl.BlockSpec((SCAN_BLK, COLW), functools.partial(
                      lambda b, j, cb: (rows(b, j), cb), cb=gate_src[1])),
                  pl.BlockSpec((1, norm_w.shape[0]), lambda b, j: (0, 0))]
    return pl.pallas_call(
        kernel,
        grid=(n_seq, 1 + n_lat_blk),
        in_specs=specs,
        out_specs=pl.BlockSpec((SCAN_BLK, COLW), lambda b, j: (rows(b, j), 0)),
        out_shape=jax.ShapeDtypeStruct((T, COLW), BF16 if finalize else F32),
        scratch_shapes=scratch,
        compiler_params=_cparams(("parallel", "arbitrary")),
        name=name,
    )(*args)


def _ret_consts(rev):
    lg = np.log(1.0 - np.power(2.0, -5.0 - np.arange(N_HEAD, dtype=np.float64)))
    i = np.arange(CHUNK, dtype=np.float64)
    steps = (CHUNK - i) if rev else (i + 1.0)
    G = steps[None, :] * lg[:, None]
    mask = _before(rev) | np.eye(CHUNK, dtype=bool)
    dm = np.where(mask[None], np.exp(G[:, :, None] - G[:, None, :]), 0.0)
    qs = np.repeat(np.exp(G).T, RET_DK, axis=1)
    ks = np.repeat(np.exp(CHUNK * lg[:, None] - G).T, RET_DK, axis=1)
    a_end = tuple(float(v) for v in np.exp(CHUNK * lg))
    return (jnp.asarray(dm, F32), jnp.asarray(qs, F32), jnp.asarray(ks, F32)), a_end


def _ret_kernel(qk_ref, qkr_ref, v_ref, cos_ref, sin_ref, dm_ref, qs_ref, ks_ref, *rest,
                rev, finalize, a_end):
    if finalize:
        oprev_ref, gate_ref, nw_ref, o_ref, s_ref = rest
    else:
        o_ref, s_ref = rest

    @pl.when(pl.program_id(1) == 0)
    def _():
        s_ref[...] = jnp.zeros_like(s_ref)

    hw = N_HEAD * RET_DK
    qk = (qk_ref[...].astype(F32) * cos_ref[...]
          + qkr_ref[...].astype(F32) * sin_ref[...])
    order = list(_chunk_order(rev))
    units = [(ci, h) for ci in range(CPB) for h in range(N_HEAD)]
    rows = lambda c: slice(c * CHUNK, (c + 1) * CHUNK)
    kcols = lambda h: slice(h * RET_DK, (h + 1) * RET_DK)
    vcols = lambda h: slice(h * RET_DV, (h + 1) * RET_DV)

    qc = [qk[rows(c), :hw] for c in order]
    kc = [qk[rows(c), hw:] for c in order]
    qb = [t.astype(BF16) for t in qc]
    kb = [t.astype(BF16) for t in kc]
    q_in = [(t * qs_ref[...]).astype(BF16) for t in qc]
    k_out = [(t * ks_ref[...]).astype(BF16) for t in kc]
    vb = [v_ref[rows(c), :].astype(BF16) for c in order]
    vh = [vb[ci][:, vcols(h)] for ci, h in units]
    sc = [(_dot_nt(qb[ci][:, kcols(h)], kb[ci][:, kcols(h)]) * dm_ref[h]).astype(BF16) for ci, h in units]
    o_intra = [_dot(s_, v_) for s_, v_ in zip(sc, vh)]
    kv = [_dot_tn(k_out[ci][:, kcols(h)], v_) for (ci, h), v_ in zip(units, vh)]

    s = [s_ref[h] for h in range(N_HEAD)]
    for ci in range(CPB):
        c = order[ci]
        for h in range(N_HEAD):
            n = ci * N_HEAD + h
            o = o_intra[n] + _dot(q_in[ci][:, kcols(h)], s[h].astype(BF16))
            s[h] = a_end[h] * s[h] + kv[n]
            if finalize:
                o = _finalize(o, oprev_ref, gate_ref, nw_ref, c * CHUNK, h * RET_DV, RET_DV)
            o_ref[rows(c), vcols(h)] = o.astype(o_ref.dtype)
    for h in range(N_HEAD):
        s_ref[h] = s[h]


def _retention(p, cos_tab, sin_tab, norm_w, n_seq, n_lat_blk):
    out = None
    for rev in (True, False):
        finalize = not rev
        consts, a_end = _ret_consts(rev)
        kern = functools.partial(_ret_kernel, rev=rev, finalize=finalize, a_end=a_end)
        out = _scan_call(kern, "retention_bwd" if rev else "retention_fwd", n_seq, n_lat_blk, rev, finalize,
                         [(p, CB_RET_QK), (p, CB_RET_QKROT), (p, CB_RET_V)], [cos_tab, sin_tab],
                         list(consts), [], out, (p, CB_RET_GATE), norm_w,
                         [pltpu.VMEM((N_HEAD, RET_DK, RET_DV), F32)])
    return out


def _rope_tables(seq_len):
    def angles(pos):
        angs = []
        for a, n in enumerate(RET_ROPE_PAIRS):
            freqs = ROPE_BASE ** (-jnp.arange(n, dtype=F32) / n)
            angs.append(pos[:, a:a + 1] * freqs)
        return jnp.concatenate(angs, axis=-1)

    t = jnp.arange(CTX_LEN, dtype=F32)
    zero = jnp.zeros((CTX_LEN,), F32)
    pos_c = jnp.stack([t, zero, zero], axis=-1)
    rows = seq_len // GRID_W
    r = jnp.repeat(jnp.arange(rows, dtype=F32), GRID_W)
    col = jnp.broadcast_to(jnp.arange(GRID_W, dtype=F32), (rows, GRID_W)).reshape(-1)
    pos_l = jnp.stack([jnp.full((seq_len,), CTX_LEN, F32), r, col], axis=-1)
    ang = jnp.concatenate([angles(pos_c), angles(pos_l)], axis=0)
    cos = jnp.concatenate([jnp.cos(ang), jnp.cos(ang)], axis=-1)
    sin = jnp.concatenate([-jnp.sin(ang), jnp.sin(ang)], axis=-1)
    scale = jnp.concatenate([jnp.ones((N_HEAD * RET_DK,), F32),
                             jnp.full((N_HEAD * RET_DK,), RET_DK ** -0.5, F32)])
    return jnp.tile(cos, (1, 2 * N_HEAD)) * scale, jnp.tile(sin, (1, 2 * N_HEAD)) * scale


def _gdn_consts(rev):
    i = np.arange(CHUNK)
    before = _before(rev)
    eye = np.eye(CHUNK, dtype=bool)
    same16 = (i[:, None] // 16) == (i[None, :] // 16)
    same32 = (i[:, None] // 32) == (i[None, :] // 32)
    masks = np.stack([before | eye, before, before & same16, before & same32 & ~same16,
                      before & ~same32, eye]).astype(np.float32)
    return jnp.asarray(masks)


def _unit_tri_inverse(a_all, m_blk, m_mid, m_top, eye):
    both = lambda f, xs, ys: [f(x, y) for x, y in zip(xs, ys)]
    a_blk = [a * m_blk for a in a_all]
    x = [eye + a for a in a_blk]
    pw = both(_dot_f32, a_blk, a_blk)
    for _ in range(2):
        x = both(lambda x_, p_: x_ + _dot_f32(x_, p_), x, pw)
        pw = both(_dot_f32, pw, pw)
    x = both(lambda x_, p_: x_ + _dot_f32(x_, p_), x, pw)
    for m in (m_mid, m_top):
        y = both(lambda x_, a_: _dot_f32(x_, a_ * m), x, a_all)
        x = both(lambda x_, y_: x_ + _dot_f32(y_, x_), x, y)
    return x


def _gdn_kernel(q_ref, k_ref, v_ref, sm_ref, ea_ref, dtb_ref, sel_ref, m_ref, *rest, rev, finalize):
    if finalize:
        oprev_ref, gate_ref, nw_ref, o_ref, s_ref = rest
    else:
        o_ref, s_ref = rest

    @pl.when(pl.program_id(1) == 0)
    def _():
        s_ref[...] = jnp.zeros_like(s_ref)

    m_incl, m_strict, m_blk, m_mid, m_top, eye = (m_ref[n] for n in range(6))
    order = list(_chunk_order(rev))
    units = [(c, h) for c in order for h in range(N_HEAD)]
    rows = lambda c: slice(c * CHUNK, (c + 1) * CHUNK)
    cols = lambda h: slice(h * GDN_DK, (h + 1) * GDN_DK)
    d = 1 if rev else 0
    lane_a = lambda h: slice(SM_A + d * N_HEAD + h, SM_A + d * N_HEAD + h + 1)
    lane_b = lambda h: slice(SM_B + d * N_HEAD + h, SM_B + d * N_HEAD + h + 1)

    x = sm_ref[...]
    xa = x + dtb_ref[...]
    g_all = -ea_ref[...] * (jnp.maximum(xa, 0.0) + jnp.log(1.0 + jnp.exp(-jnp.abs(xa))))
    beta_all = jax.nn.sigmoid(x)
    incl16 = m_incl.astype(BF16)
    G = {c: _dot_const(incl16, g_all[rows(c)], 3) for c in order}
    G_rows = {c: sum(_dot_nt(sel_ref[...], p_) for p_ in _split_bf16(G[c], 3)) for c in order}

    k16 = [k_ref[rows(c), cols(h)] for c, h in units]
    q = [q_ref[rows(c), cols(h)].astype(F32) for c, h in units]
    k = [t.astype(F32) for t in k16]
    v = [v_ref[rows(c), cols(h)].astype(F32) for c, h in units]
    g_col = [G[c][:, lane_a(h)] for c, h in units]
    beta = [beta_all[rows(c), lane_b(h)] for c, h in units]
    g_row = [G_rows[c][h:h + 1, :] for c, h in units]
    g_tot = [g[0:1] if rev else g[CHUNK - 1:CHUNK] for g in g_col]
    decay = [jnp.exp(jnp.where(m_incl > 0, gc - gr, -jnp.inf)) for gc, gr in zip(g_col, g_row)]
    k_beta = [t * b for t, b in zip(k, beta)]
    a = [-(_dot_nt(kb.astype(BF16), kk) * d) * m_strict for kb, kk, d in zip(k_beta, k16, decay)]
    t16 = [(x - eye).astype(BF16) for x in _unit_tri_inverse(a, m_blk, m_mid, m_top, eye)]
    v_beta = [t * b for t, b in zip(v, beta)]
    e_col = [jnp.exp(g) for g in g_col]
    kbg = [kb * e for kb, e in zip(k_beta, e_col)]
    u = [vb + _dot(t, vb.astype(BF16)) for t, vb in zip(t16, v_beta)]
    w16 = [(x + _dot(t, x.astype(BF16))).astype(BF16) for t, x in zip(t16, kbg)]
    sc16 = [(_dot_nt(qq.astype(BF16), kk) * d).astype(BF16) for qq, kk, d in zip(q, k16, decay)]
    q_in = [(qq * e).astype(BF16) for qq, e in zip(q, e_col)]
    k_out = [(kk * jnp.exp(gt - gc)).astype(BF16) for kk, gt, gc in zip(k, g_tot, g_col)]
    a_end = [jnp.exp(gt) for gt in g_tot]

    s = [s_ref[h] for h in range(N_HEAD)]
    for ci in range(CPB):
        us = range(ci * N_HEAD, (ci + 1) * N_HEAD)
        s16 = [t.astype(BF16) for t in s]
        vn16 = [(u[n] - _dot(w16[n], s16[h])).astype(BF16) for h, n in enumerate(us)]
        o = [_dot(q_in[n], s16[h]) + _dot(sc16[n], vn16[h]) for h, n in enumerate(us)]
        s = [a_end[n] * s[h] + _dot_tn(k_out[n], vn16[h]) for h, n in enumerate(us)]
        for h, n in enumerate(us):
            c = units[n][0]
            oh = o[h]
            if finalize:
                oh = _finalize(oh, oprev_ref, gate_ref, nw_ref, c * CHUNK, h * GDN_DV, GDN_DV)
            o_ref[rows(c), cols(h)] = oh.astype(o_ref.dtype)
    for h in range(N_HEAD):
        s_ref[h] = s[h]


PREP_SLAB = 512


def _conv_prep_kernel(z_ref, w_ref, b_ref, o_ref, *, gdn):
    L = z_ref.shape[0]
    slab = min(PREP_SLAB, L)
    cb = pl.program_id(1)
    w0, w1, w2 = w_ref[0:1, :], w_ref[1:2, :], w_ref[2:3, :]
    row = lax.broadcasted_iota(jnp.int32, (slab, z_ref.shape[1]), 0)
    zero_row = jnp.zeros((1, z_ref.shape[1]), F32)
    for r0 in range(0, L, slab):
        z = z_ref[r0:r0 + slab, :].astype(F32)
        halo = 16
        before = z_ref[r0 - halo:r0, :].astype(F32)[halo - 1:halo] if r0 > 0 else zero_row
        after = z_ref[r0 + slab:r0 + slab + halo, :].astype(F32)[0:1] if r0 + slab < L else zero_row
        prev = jnp.where(row == 0, before, pltpu.roll(z, 1, 0))
        nxt = jnp.where(row == slab - 1, after, pltpu.roll(z, slab - 1, 0))
        y = prev * w0 + z * w1 + nxt * w2
        if not gdn:
            o_ref[r0:r0 + slab, :] = (y + b_ref[...]).astype(o_ref.dtype)
            continue
        y = _silu(y)

        @pl.when(cb < 2)
        def _():
            scale = jnp.where(cb == 0, GDN_DK ** -0.5, 1.0)
            for h in range(N_HEAD):
                t = y[:, h * GDN_DK:(h + 1) * GDN_DK]
                t = t * (lax.rsqrt(jnp.sum(t * t, axis=-1, keepdims=True) + EPS) * scale)
                o_ref[r0:r0 + slab, h * GDN_DK:(h + 1) * GDN_DK] = t.astype(o_ref.dtype)

        @pl.when(cb == 2)
        def _():
            o_ref[r0:r0 + slab, :] = y.astype(o_ref.dtype)


def _conv_prep(p, col0, conv_w, conv_b, n_seq, seq_len, gdn):
    T = p.shape[0]
    dtype = BF16 if gdn else F32
    bias = jnp.zeros((1, 3 * COLW), F32) if conv_b is None else conv_b.reshape(1, -1)
    outs = []
    for L, row0 in ((seq_len, 0), (CTX_LEN, n_seq * seq_len // CTX_LEN)):
        outs.append(pl.pallas_call(
            functools.partial(_conv_prep_kernel, gdn=gdn),
            grid=(n_seq, 3),
            in_specs=[pl.BlockSpec((L, COLW), lambda b, c, row0=row0: (row0 + b, col0 + c)),
                      pl.BlockSpec((3, COLW), lambda b, c: (0, c)),
                      pl.BlockSpec((1, COLW), lambda b, c: (0, c))],
            out_specs=pl.BlockSpec((L, COLW), lambda b, c: (b, c)),
            out_shape=jax.ShapeDtypeStruct((n_seq * L, 3 * COLW), dtype),
            compiler_params=_cparams(("parallel", "parallel")),
            name="conv_prep_%s_L%d" % ("gdn" if gdn else "hyena", L),
        )(p, conv_w, bias))
    return outs


def _gdn(p, p_small, conv_w, a_log, dt_bias, norm_w, n_seq, n_lat_blk):
    T = p.shape[0]
    qkv = jnp.concatenate(_conv_prep(p, CB_GDN_Q, conv_w, None, n_seq, n_lat_blk * SCAN_BLK, True),
                          axis=0)
    pad = lambda t: jnp.pad(t.reshape(1, -1), ((0, 0), (SM_A, V7X_LANES - SM_A - 2 * N_HEAD)))
    exp_a, dtb = pad(jnp.exp(a_log)), pad(dt_bias)
    out = None
    for rev in (True, False):
        finalize = not rev
        d = 1 if rev else 0
        sel = np.zeros((16, V7X_LANES), np.float32)
        sel[np.arange(N_HEAD), SM_A + d * N_HEAD + np.arange(N_HEAD)] = 1.0
        rows = _scan_rows(n_seq, n_lat_blk, rev)
        extra = [(p_small, pl.BlockSpec((SCAN_BLK, V7X_LANES), lambda b_, j, rows=rows: (rows(b_, j), 0)))]
        kern = functools.partial(_gdn_kernel, rev=rev, finalize=finalize)
        out = _scan_call(kern, "gdn_bwd" if rev else "gdn_fwd", n_seq, n_lat_blk, rev, finalize,
                         [(qkv, 0), (qkv, 1), (qkv, 2)], [],
                         [exp_a, dtb, jnp.asarray(sel, BF16), _gdn_consts(rev)], extra,
                         out, (p, CB_GDN_GATE), norm_w,
                         [pltpu.VMEM((N_HEAD, GDN_DK, GDN_DV), F32)])
    return out


GLA_SUB = 4


def _gla_consts(rev):
    C = CHUNK
    idx = np.arange(C)
    before = _before(rev)
    incl = before | np.eye(C, dtype=bool)
    sizes = (16, 4, 1)
    blk = [idx // s for s in sizes]
    par = [idx // 64, idx // 16, idx // 4]
    sub = [b % GLA_SUB for b in blk]

    def blk_before(l, a, b):
        return (blk[l][a] > blk[l][b]) if rev else (blk[l][a] < blk[l][b])

    pq = np.zeros((2, C, C), np.float32)
    for l in range(2):
        pq[l] = incl & (blk[l][:, None] == blk[l][None, :])
    kmt = np.zeros((3, C, GLA_SUB * C), np.float32)
    maskx = np.zeros((4, C, GLA_SUB * C), np.float32)
    for l in range(3):
        for m in range(GLA_SUB):
            for j in range(C):
                tgt = par[l][j] * GLA_SUB + m
                valid = (tgt < blk[l][j]) if rev else (tgt > blk[l][j])
                if not valid:
                    continue
                t_after_j = before[:, j]
                t_blk_before_tgt = (blk[l] > tgt) if rev else (blk[l] < tgt)
                kmt[l, :, m * C + j] = t_after_j & t_blk_before_tgt
            for i in range(C):
                if sub[l][i] != m:
                    continue
                sel = (par[l] == par[l][i]) & np.array([blk_before(l, j, i) for j in range(C)])
                maskx[l, i, m * C:(m + 1) * C] = sel
    maskx[3, :, :C] = np.eye(C)
    id4 = np.tile(np.eye(C, dtype=np.float32), (1, GLA_SUB))
    return (jnp.asarray(incl.astype(np.float32), BF16), jnp.asarray(pq, BF16), jnp.asarray(kmt, BF16),
            jnp.asarray(maskx, F32), jnp.asarray(id4, BF16))


def _gla_kernel(qk_ref, v_ref, sm_ref, w2_ref, b2_ref, incl_ref, pq_ref, kmt_ref, mx_ref, id4_ref, *rest,
                rev, finalize):
    if finalize:
        oprev_ref, gate_ref, nw_ref, o_ref, st_ref = rest
    else:
        o_ref, st_ref = rest

    @pl.when(pl.program_id(1) == 0)
    def _():
        st_ref[...] = jnp.zeros_like(st_ref)

    hw = N_HEAD * GLA_DK
    x = _dot(sm_ref[...].astype(BF16), w2_ref[...]) + b2_ref[...]
    gk_all = (jnp.minimum(x, 0.0) - jnp.log(1.0 + jnp.exp(-jnp.abs(x)))) * (1.0 / GLA_TAU)
    incl = incl_ref[...]
    id4 = id4_ref[...]
    order = list(_chunk_order(rev))
    units = [(ci, h) for ci in range(CPB) for h in range(N_HEAD)]
    rows = lambda c: slice(c * CHUNK, (c + 1) * CHUNK)
    kcols = lambda h: slice(h * GLA_DK, (h + 1) * GLA_DK)
    vcols = lambda h: slice(h * GLA_DV, (h + 1) * GLA_DV)

    gk = [gk_all[rows(c)] for c in order]
    q = [qk_ref[rows(c), :hw].astype(F32) * GLA_DK ** -0.5 for c in order]
    k = [qk_ref[rows(c), hw:].astype(F32) for c in order]
    vb = [v_ref[rows(c), :].astype(BF16) for c in order]
    G = [_dot_const(incl, g, 3) for g in gk]
    g_tot = [g[0:1] if rev else g[CHUNK - 1:CHUNK] for g in G]
    e1 = [_dot_const(pq_ref[0], g, 3) for g in gk]
    e2 = [_dot_const(pq_ref[1], g, 3) for g in gk]
    q_lvl = [[(qq * jnp.exp(e)).astype(BF16) for qq, e in zip(q, es)] for es in (e1, e2, gk)]
    q_lvl.append([qq.astype(BF16) for qq in q])
    q_in = [(qq * jnp.exp(g)).astype(BF16) for qq, g in zip(q, G)]
    k_out = [(kk * jnp.exp(gt - g)).astype(BF16) for kk, gt, g in zip(k, g_tot, G)]
    kb = [kk.astype(BF16) for kk in k]
    decay_tot = [jnp.exp(gt) for gt in g_tot]

    kt_all = [_dot_tn(t, id4) for t in kb]
    kt = [kt_all[ci][kcols(h)] for ci, h in units]
    sx = [_dot(q_lvl[3][ci][:, kcols(h)], t.astype(BF16)) * mx_ref[3] for (ci, h), t in zip(units, kt)]
    for l in range(3):
        e_all = [_dot_tn_const(g, kmt_ref[l], 2) for g in gk]
        e = [e_all[ci][kcols(h)] for ci, h in units]
        kt_l = [(t * jnp.exp(e_)).astype(BF16) for t, e_ in zip(kt, e)]
        sx = [s_ + _dot(q_lvl[l][ci][:, kcols(h)], t) * mx_ref[l] for (ci, h), s_, t in zip(units, sx, kt_l)]
    vh = [vb[ci][:, vcols(h)] for ci, h in units]
    o_intra = [_dot(s_.astype(BF16), jnp.concatenate([v_] * GLA_SUB, axis=0)) for s_, v_ in zip(sx, vh)]
    kv = [_dot_tn(v_, k_out[ci][:, kcols(h)]) for (ci, h), v_ in zip(units, vh)]

    st = [st_ref[h] for h in range(N_HEAD)]
    for ci in range(CPB):
        c = order[ci]
        for h in range(N_HEAD):
            n = ci * N_HEAD + h
            o = o_intra[n] + _dot_nt(q_in[ci][:, kcols(h)], st[h].astype(BF16))
            st[h] = st[h] * decay_tot[ci][:, kcols(h)] + kv[n]
            if finalize:
                o = _finalize(o, oprev_ref, gate_ref, nw_ref, c * CHUNK, h * GLA_DV, GLA_DV)
            o_ref[rows(c), vcols(h)] = o.astype(o_ref.dtype)
    for h in range(N_HEAD):
        st_ref[h] = st[h]


def _gla(p, p_small, w2, b2, norm_w, n_seq, n_lat_blk):
    out = None
    hw = N_HEAD * GLA_DK
    for rev in (True, False):
        finalize = not rev
        d = 1 if rev else 0
        lr0 = SM_LR + d * GLA_RANK
        w2p = jnp.zeros((V7X_LANES, hw), F32).at[lr0:lr0 + GLA_RANK].set(w2[d]).astype(BF16)
        rows = _scan_rows(n_seq, n_lat_blk, rev)
        extra = [(p_small, pl.BlockSpec((SCAN_BLK, V7X_LANES), lambda b_, j, rows=rows: (rows(b_, j), 0)))]
        kern = functools.partial(_gla_kernel, rev=rev, finalize=finalize)
        out = _scan_call(kern, "gla_bwd" if rev else "gla_fwd", n_seq, n_lat_blk, rev, finalize,
                         [(p, CB_GLA_QK), (p, CB_GLA_V)], [], [w2p, b2[d].reshape(1, hw), *_gla_consts(rev)],
                         extra, out, (p, CB_GLA_GATE), norm_w,
                         [pltpu.VMEM((N_HEAD, GLA_DV, GLA_DK), F32)])
    return out


def _hyena_filters(L, w1, b1, w2, b2, w3, freq):
    t = jnp.linspace(0.0, 1.0, L, dtype=F32)[:, None]
    w = 2.0 * math.pi * jnp.arange(L, dtype=F32) / L
    f = jnp.linspace(1e-4, HY_BANDS - 1, HY_BANDS, dtype=F32)
    ang = w[:, None] * f[None, :]
    z = jnp.concatenate([t, jnp.cos(ang), -jnp.sin(ang)], axis=-1)
    hdn = jnp.sin(freq[0] * (z @ w1 + b1))
    for i in range(HY_INNER):
        hdn = jnp.sin(freq[i + 1] * (hdn @ w2[i] + b2[i]))
    h = (hdn @ w3).reshape(L, HY_ORDER, 2, HY_W)
    deltas = jnp.abs(jnp.linspace(HY_MIN_DECAY, HY_MAX_DECAY, HY_W, dtype=F32))
    h = h * jnp.exp(-t[:, :, None, None] * deltas)
    hf, hb = h[:, :, 0], h[:, :, 1]
    hb = hb.at[0].set(0.0)
    scale = lax.rsqrt(jnp.sum(hf * hf, axis=0) + jnp.sum(hb * hb, axis=0) + EPS)
    return (hf * scale).reshape(L, -1), (hb * scale).reshape(L, -1)


def _dft_mats(L):
    k = jnp.arange(L, dtype=jnp.int32)[:, None]
    n = jnp.arange(L, dtype=jnp.int32)[None, :]
    ang = (((2 * k + 1) * n) % (4 * L)).astype(F32) * (math.pi / (2 * L))
    return jnp.cos(ang).astype(BF16), jnp.sin(ang).astype(BF16)


def _hy_spectra(hf, hb, dft):
    def mm2(m, g):
        n = g.shape[1]
        out = _matmul(m, jnp.concatenate(_split_bf16(g, 2), axis=1))
        return out[:, :n] + out[:, n:]

    return mm2(dft[0], hf + hb), mm2(dft[1], hb - hf)


def _hy_kernel(z_ref, x_ref, skip_ref, c_ref, s_ref, kr_ref, ki_ref, *rest, last, n_k, slab, inv_scale):
    if last:
        gate_ref, o_ref, zb_ref, acc_ref = rest
    else:
        o_ref, zb_ref, acc_ref = rest
    kt = pl.program_id(2)

    @pl.when(kt == 0)
    def _():
        zb_ref[...] = z_ref[...].astype(BF16)
        acc_ref[...] = jnp.zeros_like(acc_ref)

    c, s = c_ref[...], s_ref[...]
    zb = zb_ref[...]
    zr, zi = _dot(c, zb), -_dot(s, zb)
    kr, ki = kr_ref[...], ki_ref[...]
    pr = (zr * kr - zi * ki).astype(BF16)
    pi = (zr * ki + zi * kr).astype(BF16)
    acc_ref[...] += _dot_tn(c, pr) - _dot_tn(s, pi)

    @pl.when(kt == n_k - 1)
    def _():
        for r0 in range(0, acc_ref.shape[0], slab):
            conv = acc_ref[r0:r0 + slab, :] * inv_scale
            z = z_ref[r0:r0 + slab, :]
            y = x_ref[r0:r0 + slab, :] * (conv + skip_ref[...] * z)
            if last:
                y = y * _silu(gate_ref[r0:r0 + slab, :].astype(F32))
            o_ref[r0:r0 + slab, :] = y.astype(o_ref.dtype)


HY_WT = 256


def _hy_order(z_src, x_src, gate_src, skip, dft, kr, ki, order, n_seq, L):
    last = gate_src is not None
    tk = min(512, L)
    n_k = L // tk
    slab = min(512, L)
    n_wt = HY_W // HY_WT

    def rowspec(src):
        _, r0, c0 = src
        return pl.BlockSpec((L, HY_WT), lambda b, w, k: (r0 + b, c0 + w), pipeline_mode=pl.Buffered(1))

    srcs = [z_src, x_src] + ([gate_src] if last else [])
    twid = pl.BlockSpec((tk, L), lambda b, w, k: (k, 0))
    spec = pl.BlockSpec((tk, HY_WT), lambda b, w, k: (k, order * n_wt + w))
    kern = functools.partial(_hy_kernel, last=last, n_k=n_k, slab=slab, inv_scale=1.0 / L)
    args = [z_src[0], x_src[0], skip[order].reshape(1, HY_W), dft[0], dft[1], kr, ki]
    specs = [rowspec(z_src), rowspec(x_src), pl.BlockSpec((1, HY_WT), lambda b, w, k: (0, w)),
             twid, twid, spec, spec]
    if last:
        args.append(gate_src[0])
        specs.append(rowspec(gate_src))
    return pl.pallas_call(
        kern,
        grid=(n_seq, n_wt, n_k),
        in_specs=specs,
        out_specs=pl.BlockSpec((L, HY_WT), lambda b, w, k: (b, w)),
        out_shape=jax.ShapeDtypeStruct((n_seq * L, HY_W), BF16 if last else F32),
        scratch_shapes=[pltpu.VMEM((L, HY_WT), BF16), pltpu.VMEM((L, HY_WT), F32)],
        compiler_params=_cparams(("parallel", "parallel", "arbitrary")),
        name="hyena_order%d_L%d" % (order, L),
    )(*args)


def _hyena(p, conv_w, conv_b, w1, b1, w2, b2, w3, freq, skip, dfts, n_seq, seq_len):
    zcs = _conv_prep(p, CB_HY_V, conv_w, conv_b, n_seq, seq_len, False)
    n_wt = HY_W // HY_WT
    outs = []
    for zc, L, row0 in zip(zcs, (seq_len, CTX_LEN), (0, n_seq * seq_len // CTX_LEN)):
        dft = dfts[L]
        kr, ki = _hy_spectra(*_hyena_filters(L, w1, b1, w2, b2, w3, freq), dft)
        y1 = _hy_order((zc, 0, 0), (zc, 0, n_wt), None, skip, dft, kr, ki, 0, n_seq, L)
        outs.append(_hy_order((y1, 0, 0), (zc, 0, 2 * n_wt), (p, row0, CB_HY_GATE * n_wt),
                              skip, dft, kr, ki, 1, n_seq, L))
    return jnp.concatenate(outs, axis=0)


def _layout_w_in(w_in):
    sizes = [3 * HY_W, HY_W, 3 * BR_W, 2 * N_HEAD, 2 * N_HEAD, BR_W, 2 * N_HEAD * RET_DK, BR_W, BR_W,
             2 * N_HEAD * GLA_DK, BR_W, 2 * GLA_RANK, BR_W]
    offs = np.concatenate([[0], np.cumsum(sizes)])
    (hy_proj, hy_gate, gdn_qkv, gdn_a, gdn_b, gdn_gate, ret_qk, ret_v, ret_gate,
     gla_qk, gla_v, gla_lr, gla_gate) = [w_in[:, offs[n]:offs[n + 1]] for n in range(len(sizes))]
    D = w_in.shape[0]
    half = RET_DK // 2
    ret_rot = ret_qk.reshape(D, 2 * N_HEAD, 2, half)[:, :, ::-1].reshape(D, -1)
    main = jnp.concatenate([hy_proj, hy_gate, gdn_qkv, gdn_gate, ret_qk, ret_rot, ret_v, ret_gate,
                            gla_qk, gla_v, gla_gate], axis=1)
    small = jnp.concatenate([gdn_a, gdn_b, gla_lr], axis=1)
    small = jnp.pad(small, ((0, 0), (0, V7X_LANES - small.shape[1])))
    return main, small


def _layer(x, mod, dims, rope, dfts, norm_w, w_in, hy_conv_w, hy_conv_b, hy_w1, hy_b1, hy_w2, hy_b2, hy_w3,
           hy_freq, hy_skip, gdn_conv_w, gdn_a_log, gdn_dt_bias, gdn_norm, ret_norm, gla_w2, gla_b2,
           gla_norm, w_branch, w_merge, b_merge, w_out):
    n_seq, seq_len = dims
    n_lat_rows = n_seq * seq_len
    n_lat_blk = seq_len // SCAN_BLK
    shift, scale, gate = mod
    w_main, w_small = _layout_w_in(w_in.astype(BF16))
    p, h = _inproj(x, norm_w, scale, shift, w_main, n_lat_rows, seq_len)
    p_small = _matmul(h, w_small)

    y_hy = _hyena(p, hy_conv_w, hy_conv_b, hy_w1, hy_b1, hy_w2, hy_b2, hy_w3, hy_freq, hy_skip, dfts,
                  n_seq, seq_len)
    y_gdn = _gdn(p, p_small, gdn_conv_w, gdn_a_log, gdn_dt_bias, gdn_norm, n_seq, n_lat_blk)
    y_ret = _retention(p, rope[0], rope[1], ret_norm, n_seq, n_lat_blk)
    y_gla = _gla(p, p_small, gla_w2, gla_b2, gla_norm, n_seq, n_lat_blk)

    merged = _merge(h, (y_hy, y_gdn, y_ret, y_gla), w_merge.astype(BF16), b_merge, w_branch.astype(BF16))
    return _outproj(merged, w_out.astype(BF16), x, gate, n_lat_rows, seq_len)


def kernel(x, c, ctx, c_ctx, norm_w, ada_w, ada_b, w_in, hy_conv_w, hy_conv_b, hy_w1, hy_b1,
           hy_w2, hy_b2, hy_w3, hy_freq, hy_skip, gdn_conv_w, gdn_a_log, gdn_dt_bias,
           gdn_norm, ret_norm, gla_w2, gla_b2, gla_norm, w_branch, w_merge, b_merge, w_out,
           final_norm):
    B, S, D = x.shape
    layer_params = (norm_w, w_in, hy_conv_w, hy_conv_b, hy_w1, hy_b1, hy_w2, hy_b2, hy_w3, hy_freq,
                    hy_skip, gdn_conv_w, gdn_a_log, gdn_dt_bias, gdn_norm, ret_norm, gla_w2, gla_b2,
                    gla_norm, w_branch, w_merge, b_merge, w_out)
    xt = jnp.concatenate([x.reshape(-1, D), ctx.reshape(-1, D)], axis=0)
    cond = _silu(jnp.concatenate([c_ctx[None], c], axis=0))
    cond = jnp.pad(cond, ((0, -(1 + B) % 8), (0, 0))).astype(BF16)
    rope = _rope_tables(S)
    dfts = {L: _dft_mats(L) for L in (S, CTX_LEN)}
    for l in range(DEPTH):
        m = (_matmul(cond, ada_w[l].astype(BF16)) + ada_b[l])[:1 + B]
        mod = tuple(t.reshape(1 + B, 1, D) for t in jnp.split(m, 3, axis=-1))
        xt = _layer(xt, mod, (B, S), rope, dfts, *[p[l] for p in layer_params])
    return _final_norm(xt, final_norm, B * S).reshape(B, S, D)
```

```python
import functools
import math

import numpy as np
import jax
import jax.numpy as jnp
from jax import lax
from jax.experimental import pallas as pl
from jax.experimental.pallas import tpu as pltpu

D_MODEL = 2048
DEPTH = 4
CTX_LEN = 256
GRID_W = 64
F32 = jnp.float32
BF16 = jnp.bfloat16
EPS = 1e-6
CHUNK = 64

N_BRANCH = 4
BR_W = D_MODEL // N_BRANCH

HY_W = BR_W
HY_ORDER = 2
HY_EMB = 33
HY_BANDS = (HY_EMB - 1) // 2
HY_INNER = 2
HY_FAST_DECAY = 0.3
HY_SLOW_DECAY = 1.5
HY_TARGET = 1e-2
HY_MIN_DECAY = math.log(HY_TARGET) / HY_SLOW_DECAY
HY_MAX_DECAY = math.log(HY_TARGET) / HY_FAST_DECAY

N_HEAD = 4
GDN_DK = BR_W // N_HEAD
GDN_DV = BR_W // N_HEAD
RET_DK = BR_W // (2 * N_HEAD)
RET_DV = BR_W // N_HEAD
RET_ROPE_PAIRS = (8, 12, 12)
ROPE_BASE = 10000.0
GLA_DK = BR_W // (2 * N_HEAD)
GLA_DV = BR_W // N_HEAD
GLA_RANK = 16
GLA_TAU = 16.0

V7X_LANES = 128
VMEM_LIMIT_BYTES = 56 * 1024 * 1024

SCAN_BLK = CTX_LEN
CPB = SCAN_BLK // CHUNK
COLW = BR_W

CB_HY_V, CB_HY_X1, CB_HY_X2, CB_HY_GATE = 0, 1, 2, 3
CB_GDN_Q, CB_GDN_GATE = 4, 7
CB_RET_QK, CB_RET_QKROT, CB_RET_V, CB_RET_GATE = 8, 9, 10, 11
CB_GLA_QK, CB_GLA_V, CB_GLA_GATE = 12, 13, 14
N_COLBLK = 15
SM_A, SM_B, SM_LR = 0, 2 * N_HEAD, 4 * N_HEAD


def _dot(a, b):
    return jnp.dot(a, b, preferred_element_type=F32)


def _dot_nt(a, b):
    return lax.dot_general(a, b, (((1,), (1,)), ((), ())), preferred_element_type=F32)


def _dot_tn(a, b):
    return lax.dot_general(a, b, (((0,), (0,)), ((), ())), preferred_element_type=F32)


def _split_bf16(x, terms):
    parts = []
    r = x
    for t in range(terms):
        p = r.astype(BF16)
        parts.append(p)
        if t + 1 < terms:
            r = r - p.astype(F32)
    return parts


def _dot_f32(a, b):
    a1, a2 = _split_bf16(a, 2)
    b1, b2 = _split_bf16(b, 2)
    return _dot(a1, b1) + (_dot(a1, b2) + _dot(a2, b1))


def _dot_const(c, x, terms):
    out = None
    for p in _split_bf16(x, terms):
        t = _dot(c, p)
        out = t if out is None else out + t
    return out


def _dot_tn_const(x, c, terms):
    out = None
    for p in _split_bf16(x, terms):
        t = _dot_tn(p, c)
        out = t if out is None else out + t
    return out


def _silu(x):
    return x * jax.nn.sigmoid(x)


def _cparams(sem):
    return pltpu.CompilerParams(dimension_semantics=sem, vmem_limit_bytes=VMEM_LIMIT_BYTES)


def _mm_kernel(a_ref, b_ref, o_ref):
    o_ref[...] = _dot(a_ref[...], b_ref[...]).astype(o_ref.dtype)


def _pick_tile(n, cands):
    for c in cands:
        if n % c == 0:
            return c
    return n


def _matmul(a, b, out_dtype=F32):
    M, K = a.shape
    _, N = b.shape
    tm = _pick_tile(M, (1024, 512, 256, 128, 64, 32, 16, 8))
    tn = _pick_tile(N, (512, 384, 256, 128))
    return pl.pallas_call(
        _mm_kernel,
        grid=(M // tm, N // tn),
        in_specs=[pl.BlockSpec((tm, K), lambda i, j: (i, 0)),
                  pl.BlockSpec((K, tn), lambda i, j: (0, j))],
        out_specs=pl.BlockSpec((tm, tn), lambda i, j: (i, j)),
        out_shape=jax.ShapeDtypeStruct((M, N), out_dtype),
        compiler_params=_cparams(("parallel", "arbitrary")),
        name="matmul",
    )(a, b)


def _mod_row(n_lat_blk, blk_per_seq):
    return lambda i: jnp.where(i < n_lat_blk, 1 + i // blk_per_seq, 0)


def _inproj_kernel(x_ref, nw_ref, sc_ref, sh_ref, w_ref, p_ref, h_ref, hs_ref):
    @pl.when(pl.program_id(1) == 0)
    def _():
        for r0 in range(0, x_ref.shape[0], PREP_SLAB):
            x = x_ref[r0:r0 + PREP_SLAB, :]
            y = x * lax.rsqrt(jnp.mean(x * x, axis=-1, keepdims=True) + EPS) * nw_ref[...]
            h = (y * (1.0 + sc_ref[0]) + sh_ref[0]).astype(BF16)
            hs_ref[r0:r0 + PREP_SLAB, :] = h
            h_ref[r0:r0 + PREP_SLAB, :] = h

    p_ref[...] = _dot(hs_ref[...], w_ref[...]).astype(p_ref.dtype)


def _inproj(x, norm_w, scale, shift, w, n_lat_rows, seq_len):
    T, D = x.shape
    N = w.shape[1]
    tm, tn = 1024, 3 * COLW
    row = _mod_row(n_lat_rows // tm, seq_len // tm)
    return pl.pallas_call(
        _inproj_kernel,
        grid=(T // tm, N // tn),
        in_specs=[pl.BlockSpec((tm, D), lambda i, j: (i, 0)),
                  pl.BlockSpec((1, D), lambda i, j: (0, 0)),
                  pl.BlockSpec((1, 1, D), lambda i, j: (row(i), 0, 0)),
                  pl.BlockSpec((1, 1, D), lambda i, j: (row(i), 0, 0)),
                  pl.BlockSpec((D, tn), lambda i, j: (0, j))],
        out_specs=[pl.BlockSpec((tm, tn), lambda i, j: (i, j)),
                   pl.BlockSpec((tm, D), lambda i, j: (i, 0))],
        out_shape=[jax.ShapeDtypeStruct((T, N), BF16), jax.ShapeDtypeStruct((T, D), BF16)],
        scratch_shapes=[pltpu.VMEM((tm, D), BF16)],
        compiler_params=_cparams(("parallel", "arbitrary")),
        name="inproj",
    )(x, norm_w.reshape(1, D), scale, shift, w)


def _merge_kernel(h_ref, y0_ref, y1_ref, y2_ref, y3_ref, wm_ref, bm_ref, wb_ref, o_ref):
    h = h_ref[...]
    acc = None
    for i, y_ref in enumerate((y0_ref, y1_ref, y2_ref, y3_ref)):
        g = jax.nn.sigmoid(_dot(h, wm_ref[i]) + bm_ref[i])
        t = g * _dot(y_ref[...], wb_ref[i])
        acc = t if acc is None else acc + t
    o_ref[...] = acc.astype(o_ref.dtype)


def _merge(h, ys, w_merge, b_merge, w_branch):
    T, D = h.shape
    tm, tn = 1024, 512
    yspec = pl.BlockSpec((tm, BR_W), lambda i, j: (i, 0))
    return pl.pallas_call(
        _merge_kernel,
        grid=(T // tm, D // tn),
        in_specs=[pl.BlockSpec((tm, D), lambda i, j: (i, 0)), yspec, yspec, yspec, yspec,
                  pl.BlockSpec((N_BRANCH, D, tn), lambda i, j: (0, 0, j)),
                  pl.BlockSpec((N_BRANCH, 1, tn), lambda i, j: (0, 0, j)),
                  pl.BlockSpec((N_BRANCH, BR_W, tn), lambda i, j: (0, 0, j))],
        out_specs=pl.BlockSpec((tm, tn), lambda i, j: (i, j)),
        out_shape=jax.ShapeDtypeStruct((T, D), BF16),
        compiler_params=_cparams(("parallel", "arbitrary")),
        name="merge",
    )(h, *ys, w_merge, b_merge.reshape(N_BRANCH, 1, D), w_branch)


def _outproj_kernel(a_ref, w_ref, x_ref, g_ref, o_ref):
    o_ref[...] = x_ref[...] + g_ref[0] * _dot(a_ref[...], w_ref[...])


def _outproj(a, w, x, gate, n_lat_rows, seq_len):
    T, D = x.shape
    tm, tn = 1024, 1024
    row = _mod_row(n_lat_rows // tm, seq_len // tm)
    return pl.pallas_call(
        _outproj_kernel,
        grid=(T // tm, D // tn),
        in_specs=[pl.BlockSpec((tm, D), lambda i, j: (i, 0)),
                  pl.BlockSpec((D, tn), lambda i, j: (0, j)),
                  pl.BlockSpec((tm, tn), lambda i, j: (i, j)),
                  pl.BlockSpec((1, 1, tn), lambda i, j: (row(i), 0, j))],
        out_specs=pl.BlockSpec((tm, tn), lambda i, j: (i, j)),
        out_shape=jax.ShapeDtypeStruct((T, D), F32),
        compiler_params=_cparams(("parallel", "arbitrary")),
        name="outproj",
    )(a, w, x, gate)


def _final_norm_kernel(x_ref, w_ref, o_ref):
    x = x_ref[...]
    o_ref[...] = x * lax.rsqrt(jnp.mean(x * x, axis=-1, keepdims=True) + EPS) * w_ref[...]


def _final_norm(x, w, n_lat_rows):
    D = x.shape[1]
    tm = 512
    return pl.pallas_call(
        _final_norm_kernel,
        grid=(n_lat_rows // tm,),
        in_specs=[pl.BlockSpec((tm, D), lambda i: (i, 0)),
                  pl.BlockSpec((1, D), lambda i: (0, 0))],
        out_specs=pl.BlockSpec((tm, D), lambda i: (i, 0)),
        out_shape=jax.ShapeDtypeStruct((n_lat_rows, D), F32),
        compiler_params=_cparams(("parallel",)),
        name="final_norm",
    )(x, w.reshape(1, D))


def _scan_rows(n_seq, n_lat_blk, rev):
    def idx(b, j):
        jj = (n_lat_blk - j) if rev else (j - 1)
        return jnp.where(j == 0, n_seq * n_lat_blk + b, b * n_lat_blk + jj)
    return idx


def _chunk_order(rev):
    return range(CPB - 1, -1, -1) if rev else range(CPB)


def _before(rev):
    i = np.arange(CHUNK)[:, None]
    j = np.arange(CHUNK)[None, :]
    return (j > i) if rev else (j < i)


def _finalize(o, oprev_ref, gate_ref, nw_ref, r0, c0, width):
    o = o + oprev_ref[r0:r0 + CHUNK, c0:c0 + width]
    o = o * lax.rsqrt(jnp.mean(o * o, axis=-1, keepdims=True) + EPS) * nw_ref[...]
    return o * _silu(gate_ref[r0:r0 + CHUNK, c0:c0 + width].astype(F32))


def _scan_call(kernel, name, n_seq, n_lat_blk, rev, finalize, row_inputs, tab_inputs, const_inputs,
               extra_specs_inputs, oprev, gate_src, norm_w, scratch):
    rows = _scan_rows(n_seq, n_lat_blk, rev)
    T = row_inputs[0][0].shape[0]
    args, specs = [], []
    for arr, cb in row_inputs:
        args.append(arr)
        specs.append(pl.BlockSpec((SCAN_BLK, COLW), functools.partial(
            lambda b, j, cb: (rows(b, j), cb), cb=cb)))
    for arr in tab_inputs:
        args.append(arr)
        specs.append(pl.BlockSpec((SCAN_BLK, arr.shape[1]), lambda b, j: (
            jnp.where(j == 0, 0, 1 + ((n_lat_blk - j) if rev else (j - 1))), 0)))
    for arr, spec in extra_specs_inputs:
        args.append(arr)
        specs.append(spec)
    for arr in const_inputs:
        args.append(arr)
        specs.append(pl.BlockSpec(arr.shape, functools.partial(lambda b, j, n: (0,) * n, n=arr.ndim)))
    if finalize:
        args += [oprev, gate_src[0], norm_w.reshape(1, -1)]
        specs += [pl.BlockSpec((SCAN_BLK, COLW), lambda b, j: (rows(b, j), 0)),
                  pl.BlockSpec((SCAN_BLK, COLW), functools.partial(
                      lambda b, j, cb: (rows(b, j), cb), cb=gate_src[1])),
                  pl.BlockSpec((1, norm_w.shape[0]), lambda b, j: (0, 0))]
    return pl.pallas_call(
        kernel,
        grid=(n_seq, 1 + n_lat_blk),
        in_specs=specs,
        out_specs=pl.BlockSpec((SCAN_BLK, COLW), lambda b, j: (rows(b, j), 0)),
        out_shape=jax.ShapeDtypeStruct((T, COLW), BF16 if finalize else F32),
        scratch_shapes=scratch,
        compiler_params=_cparams(("parallel", "arbitrary")),
        name=name,
    )(*args)


def _ret_consts(rev):
    lg = np.log(1.0 - np.power(2.0, -5.0 - np.arange(N_HEAD, dtype=np.float64)))
    i = np.arange(CHUNK, dtype=np.float64)
    steps = (CHUNK - i) if rev else (i + 1.0)
    G = steps[None, :] * lg[:, None]
    mask = _before(rev) | np.eye(CHUNK, dtype=bool)
    dm = np.where(mask[None], np.exp(G[:, :, None] - G[:, None, :]), 0.0)
    qs = np.repeat(np.exp(G).T, RET_DK, axis=1)
    ks = np.repeat(np.exp(CHUNK * lg[:, None] - G).T, RET_DK, axis=1)
    a_end = tuple(float(v) for v in np.exp(CHUNK * lg))
    return (jnp.asarray(dm, F32), jnp.asarray(qs, F32), jnp.asarray(ks, F32)), a_end


def _ret_kernel(qk_ref, qkr_ref, v_ref, cos_ref, sin_ref, dm_ref, qs_ref, ks_ref, *rest,
                rev, finalize, a_end):
    if finalize:
        oprev_ref, gate_ref, nw_ref, o_ref, s_ref = rest
    else:
        o_ref, s_ref = rest

    @pl.when(pl.program_id(1) == 0)
    def _():
        s_ref[...] = jnp.zeros_like(s_ref)

    hw = N_HEAD * RET_DK
    qk = (qk_ref[...].astype(F32) * cos_ref[...]
          + qkr_ref[...].astype(F32) * sin_ref[...])
    order = list(_chunk_order(rev))
    units = [(ci, h) for ci in range(CPB) for h in range(N_HEAD)]
    rows = lambda c: slice(c * CHUNK, (c + 1) * CHUNK)
    kcols = lambda h: slice(h * RET_DK, (h + 1) * RET_DK)
    vcols = lambda h: slice(h * RET_DV, (h + 1) * RET_DV)

    qc = [qk[rows(c), :hw] for c in order]
    kc = [qk[rows(c), hw:] for c in order]
    qb = [t.astype(BF16) for t in qc]
    kb = [t.astype(BF16) for t in kc]
    q_in = [(t * qs_ref[...]).astype(BF16) for t in qc]
    k_out = [(t * ks_ref[...]).astype(BF16) for t in kc]
    vb = [v_ref[rows(c), :].astype(BF16) for c in order]
    vh = [vb[ci][:, vcols(h)] for ci, h in units]
    sc = [(_dot_nt(qb[ci][:, kcols(h)], kb[ci][:, kcols(h)]) * dm_ref[h]).astype(BF16) for ci, h in units]
    o_intra = [_dot(s_, v_) for s_, v_ in zip(sc, vh)]
    kv = [_dot_tn(k_out[ci][:, kcols(h)], v_) for (ci, h), v_ in zip(units, vh)]

    s = [s_ref[h] for h in range(N_HEAD)]
    for ci in range(CPB):
        c = order[ci]
        for h in range(N_HEAD):
            n = ci * N_HEAD + h
            o = o_intra[n] + _dot(q_in[ci][:, kcols(h)], s[h].astype(BF16))
            s[h] = a_end[h] * s[h] + kv[n]
            if finalize:
                o = _finalize(o, oprev_ref, gate_ref, nw_ref, c * CHUNK, h * RET_DV, RET_DV)
            o_ref[rows(c), vcols(h)] = o.astype(o_ref.dtype)
    for h in range(N_HEAD):
        s_ref[h] = s[h]


def _retention(p, cos_tab, sin_tab, norm_w, n_seq, n_lat_blk):
    out = None
    for rev in (True, False):
        finalize = not rev
        consts, a_end = _ret_consts(rev)
        kern = functools.partial(_ret_kernel, rev=rev, finalize=finalize, a_end=a_end)
        out = _scan_call(kern, "retention_bwd" if rev else "retention_fwd", n_seq, n_lat_blk, rev, finalize,
                         [(p, CB_RET_QK), (p, CB_RET_QKROT), (p, CB_RET_V)], [cos_tab, sin_tab],
                         list(consts), [], out, (p, CB_RET_GATE), norm_w,
                         [pltpu.VMEM((N_HEAD, RET_DK, RET_DV), F32)])
    return out


def _rope_tables(seq_len):
    def angles(pos):
        angs = []
        for a, n in enumerate(RET_ROPE_PAIRS):
            freqs = ROPE_BASE ** (-jnp.arange(n, dtype=F32) / n)
            angs.append(pos[:, a:a + 1] * freqs)
        return jnp.concatenate(angs, axis=-1)

    t = jnp.arange(CTX_LEN, dtype=F32)
    zero = jnp.zeros((CTX_LEN,), F32)
    pos_c = jnp.stack([t, zero, zero], axis=-1)
    rows = seq_len // GRID_W
    r = jnp.repeat(jnp.arange(rows, dtype=F32), GRID_W)
    col = jnp.broadcast_to(jnp.arange(GRID_W, dtype=F32), (rows, GRID_W)).reshape(-1)
    pos_l = jnp.stack([jnp.full((seq_len,), CTX_LEN, F32), r, col], axis=-1)
    ang = jnp.concatenate([angles(pos_c), angles(pos_l)], axis=0)
    cos = jnp.concatenate([jnp.cos(ang), jnp.cos(ang)], axis=-1)
    sin = jnp.concatenate([-jnp.sin(ang), jnp.sin(ang)], axis=-1)
    scale = jnp.concatenate([jnp.ones((N_HEAD * RET_DK,), F32),
                             jnp.full((N_HEAD * RET_DK,), RET_DK ** -0.5, F32)])
    return jnp.tile(cos, (1, 2 * N_HEAD)) * scale, jnp.tile(sin, (1, 2 * N_HEAD)) * scale


def _gdn_consts(rev):
    i = np.arange(CHUNK)
    before = _before(rev)
    eye = np.eye(CHUNK, dtype=bool)
    same16 = (i[:, None] // 16) == (i[None, :] // 16)
    same32 = (i[:, None] // 32) == (i[None, :] // 32)
    masks = np.stack([before | eye, before, before & same16, before & same32 & ~same16,
                      before & ~same32, eye]).astype(np.float32)
    return jnp.asarray(masks)


def _unit_tri_inverse(a_all, m_blk, m_mid, m_top, eye):
    both = lambda f, xs, ys: [f(x, y) for x, y in zip(xs, ys)]
    a_blk = [a * m_blk for a in a_all]
    x = [eye + a for a in a_blk]
    pw = both(_dot_f32, a_blk, a_blk)
    for _ in range(2):
        x = both(lambda x_, p_: x_ + _dot_f32(x_, p_), x, pw)
        pw = both(_dot_f32, pw, pw)
    x = both(lambda x_, p_: x_ + _dot_f32(x_, p_), x, pw)
    for m in (m_mid, m_top):
        y = both(lambda x_, a_: _dot_f32(x_, a_ * m), x, a_all)
        x = both(lambda x_, y_: x_ + _dot_f32(y_, x_), x, y)
    return x


def _gdn_kernel(q_ref, k_ref, v_ref, sm_ref, ea_ref, dtb_ref, sel_ref, m_ref, *rest, rev, finalize):
    if finalize:
        oprev_ref, gate_ref, nw_ref, o_ref, s_ref = rest
    else:
        o_ref, s_ref = rest

    @pl.when(pl.program_id(1) == 0)
    def _():
        s_ref[...] = jnp.zeros_like(s_ref)

    m_incl, m_strict, m_blk, m_mid, m_top, eye = (m_ref[n] for n in range(6))
    order = list(_chunk_order(rev))
    units = [(c, h) for c in order for h in range(N_HEAD)]
    rows = lambda c: slice(c * CHUNK, (c + 1) * CHUNK)
    cols = lambda h: slice(h * GDN_DK, (h + 1) * GDN_DK)
    d = 1 if rev else 0
    lane_a = lambda h: slice(SM_A + d * N_HEAD + h, SM_A + d * N_HEAD + h + 1)
    lane_b = lambda h: slice(SM_B + d * N_HEAD + h, SM_B + d * N_HEAD + h + 1)

    x = sm_ref[...]
    xa = x + dtb_ref[...]
    g_all = -ea_ref[...] * (jnp.maximum(xa, 0.0) + jnp.log(1.0 + jnp.exp(-jnp.abs(xa))))
    beta_all = jax.nn.sigmoid(x)
    incl16 = m_incl.astype(BF16)
    G = {c: _dot_const(incl16, g_all[rows(c)], 3) for c in order}
    G_rows = {c: sum(_dot_nt(sel_ref[...], p_) for p_ in _split_bf16(G[c], 3)) for c in order}

    k16 = [k_ref[rows(c), cols(h)] for c, h in units]
    q = [q_ref[rows(c), cols(h)].astype(F32) for c, h in units]
    k = [t.astype(F32) for t in k16]
    v = [v_ref[rows(c), cols(h)].astype(F32) for c, h in units]
    g_col = [G[c][:, lane_a(h)] for c, h in units]
    beta = [beta_all[rows(c), lane_b(h)] for c, h in units]
    g_row = [G_rows[c][h:h + 1, :] for c, h in units]
    g_tot = [g[0:1] if rev else g[CHUNK - 1:CHUNK] for g in g_col]
    decay = [jnp.exp(jnp.where(m_incl > 0, gc - gr, -jnp.inf)) for gc, gr in zip(g_col, g_row)]
    k_beta = [t * b for t, b in zip(k, beta)]
    a = [-(_dot_nt(kb.astype(BF16), kk) * d) * m_strict for kb, kk, d in zip(k_beta, k16, decay)]
    t16 = [(x - eye).astype(BF16) for x in _unit_tri_inverse(a, m_blk, m_mid, m_top, eye)]
    v_beta = [t * b for t, b in zip(v, beta)]
    e_col = [jnp.exp(g) for g in g_col]
    kbg = [kb * e for kb, e in zip(k_beta, e_col)]
    u = [vb + _dot(t, vb.astype(BF16)) for t, vb in zip(t16, v_beta)]
    w16 = [(x + _dot(t, x.astype(BF16))).astype(BF16) for t, x in zip(t16, kbg)]
    sc16 = [(_dot_nt(qq.astype(BF16), kk) * d).astype(BF16) for qq, kk, d in zip(q, k16, decay)]
    q_in = [(qq * e).astype(BF16) for qq, e in zip(q, e_col)]
    k_out = [(kk * jnp.exp(gt - gc)).astype(BF16) for kk, gt, gc in zip(k, g_tot, g_col)]
    a_end = [jnp.exp(gt) for gt in g_tot]

    s = [s_ref[h] for h in range(N_HEAD)]
    for ci in range(CPB):
        us = range(ci * N_HEAD, (ci + 1) * N_HEAD)
        s16 = [t.astype(BF16) for t in s]
        vn16 = [(u[n] - _dot(w16[n], s16[h])).astype(BF16) for h, n in enumerate(us)]
        o = [_dot(q_in[n], s16[h]) + _dot(sc16[n], vn16[h]) for h, n in enumerate(us)]
        s = [a_end[n] * s[h] + _dot_tn(k_out[n], vn16[h]) for h, n in enumerate(us)]
        for h, n in enumerate(us):
            c = units[n][0]
            oh = o[h]
            if finalize:
                oh = _finalize(oh, oprev_ref, gate_ref, nw_ref, c * CHUNK, h * GDN_DV, GDN_DV)
            o_ref[rows(c), cols(h)] = oh.astype(o_ref.dtype)
    for h in range(N_HEAD):
        s_ref[h] = s[h]


PREP_SLAB = 512


def _conv_prep_kernel(z_ref, w_ref, b_ref, o_ref, *, gdn):
    L = z_ref.shape[0]
    slab = min(PREP_SLAB, L)
    cb = pl.program_id(1)
    w0, w1, w2 = w_ref[0:1, :], w_ref[1:2, :], w_ref[2:3, :]
    row = lax.broadcasted_iota(jnp.int32, (slab, z_ref.shape[1]), 0)
    zero_row = jnp.zeros((1, z_ref.shape[1]), F32)
    for r0 in range(0, L, slab):
        z = z_ref[r0:r0 + slab, :].astype(F32)
        halo = 16
        before = z_ref[r0 - halo:r0, :].astype(F32)[halo - 1:halo] if r0 > 0 else zero_row
        after = z_ref[r0 + slab:r0 + slab + halo, :].astype(F32)[0:1] if r0 + slab < L else zero_row
        prev = jnp.where(row == 0, before, pltpu.roll(z, 1, 0))
        nxt = jnp.where(row == slab - 1, after, pltpu.roll(z, slab - 1, 0))
        y = prev * w0 + z * w1 + nxt * w2
        if not gdn:
            o_ref[r0:r0 + slab, :] = (y + b_ref[...]).astype(o_ref.dtype)
            continue
        y = _silu(y)

        @pl.when(cb < 2)
        def _():
            scale = jnp.where(cb == 0, GDN_DK ** -0.5, 1.0)
            for h in range(N_HEAD):
                t = y[:, h * GDN_DK:(h + 1) * GDN_DK]
                t = t * (lax.rsqrt(jnp.sum(t * t, axis=-1, keepdims=True) + EPS) * scale)
                o_ref[r0:r0 + slab, h * GDN_DK:(h + 1) * GDN_DK] = t.astype(o_ref.dtype)

        @pl.when(cb == 2)
        def _():
            o_ref[r0:r0 + slab, :] = y.astype(o_ref.dtype)


def _conv_prep(p, col0, conv_w, conv_b, n_seq, seq_len, gdn):
    bias = jnp.zeros((1, 3 * COLW), F32) if conv_b is None else conv_b.reshape(1, -1)
    outs = []
    for L, row0 in ((seq_len, 0), (CTX_LEN, n_seq * seq_len // CTX_LEN)):
        outs.append(pl.pallas_call(
            functools.partial(_conv_prep_kernel, gdn=gdn),
            grid=(n_seq, 3),
            in_specs=[pl.BlockSpec((L, COLW), lambda b, c, row0=row0: (row0 + b, col0 + c)),
                      pl.BlockSpec((3, COLW), lambda b, c: (0, c)),
                      pl.BlockSpec((1, COLW), lambda b, c: (0, c))],
            out_specs=pl.BlockSpec((L, COLW), lambda b, c: (b, c)),
            out_shape=jax.ShapeDtypeStruct((n_seq * L, 3 * COLW), BF16),
            compiler_params=_cparams(("parallel", "parallel")),
            name="conv_prep_%s_L%d" % ("gdn" if gdn else "hyena", L),
        )(p, conv_w, bias))
    return outs


def _gdn(p, p_small, conv_w, a_log, dt_bias, norm_w, n_seq, n_lat_blk):
    T = p.shape[0]
    qkv = jnp.concatenate(_conv_prep(p, CB_GDN_Q, conv_w, None, n_seq, n_lat_blk * SCAN_BLK, True),
                          axis=0)
    pad = lambda t: jnp.pad(t.reshape(1, -1), ((0, 0), (SM_A, V7X_LANES - SM_A - 2 * N_HEAD)))
    exp_a, dtb = pad(jnp.exp(a_log)), pad(dt_bias)
    out = None
    for rev in (True, False):
        finalize = not rev
        d = 1 if rev else 0
        sel = np.zeros((16, V7X_LANES), np.float32)
        sel[np.arange(N_HEAD), SM_A + d * N_HEAD + np.arange(N_HEAD)] = 1.0
        rows = _scan_rows(n_seq, n_lat_blk, rev)
        extra = [(p_small, pl.BlockSpec((SCAN_BLK, V7X_LANES), lambda b_, j, rows=rows: (rows(b_, j), 0)))]
        kern = functools.partial(_gdn_kernel, rev=rev, finalize=finalize)
        out = _scan_call(kern, "gdn_bwd" if rev else "gdn_fwd", n_seq, n_lat_blk, rev, finalize,
                         [(qkv, 0), (qkv, 1), (qkv, 2)], [],
                         [exp_a, dtb, jnp.asarray(sel, BF16), _gdn_consts(rev)], extra,
                         out, (p, CB_GDN_GATE), norm_w,
                         [pltpu.VMEM((N_HEAD, GDN_DK, GDN_DV), F32)])
    return out


GLA_SUB = 4


def _gla_consts(rev):
    C = CHUNK
    idx = np.arange(C)
    before = _before(rev)
    incl = before | np.eye(C, dtype=bool)
    sizes = (16, 4, 1)
    blk = [idx // s for s in sizes]
    par = [idx // 64, idx // 16, idx // 4]
    sub = [b % GLA_SUB for b in blk]

    def blk_before(l, a, b):
        return (blk[l][a] > blk[l][b]) if rev else (blk[l][a] < blk[l][b])

    pq = np.zeros((2, C, C), np.float32)
    for l in range(2):
        pq[l] = incl & (blk[l][:, None] == blk[l][None, :])
    kmt = np.zeros((3, C, GLA_SUB * C), np.float32)
    maskx = np.zeros((4, C, GLA_SUB * C), np.float32)
    for l in range(3):
        for m in range(GLA_SUB):
            for j in range(C):
                tgt = par[l][j] * GLA_SUB + m
                valid = (tgt < blk[l][j]) if rev else (tgt > blk[l][j])
                if not valid:
                    continue
                t_after_j = before[:, j]
                t_blk_before_tgt = (blk[l] > tgt) if rev else (blk[l] < tgt)
                kmt[l, :, m * C + j] = t_after_j & t_blk_before_tgt
            for i in range(C):
                if sub[l][i] != m:
                    continue
                sel = (par[l] == par[l][i]) & np.array([blk_before(l, j, i) for j in range(C)])
                maskx[l, i, m * C:(m + 1) * C] = sel
    maskx[3, :, :C] = np.eye(C)
    id4 = np.tile(np.eye(C, dtype=np.float32), (1, GLA_SUB))
    return (jnp.asarray(incl.astype(np.float32), BF16), jnp.asarray(pq, BF16), jnp.asarray(kmt, BF16),
            jnp.asarray(maskx, F32), jnp.asarray(id4, BF16))


def _gla_kernel(qk_ref, v_ref, sm_ref, w2_ref, b2_ref, incl_ref, pq_ref, kmt_ref, mx_ref, id4_ref, *rest,
                rev, finalize):
    if finalize:
        oprev_ref, gate_ref, nw_ref, o_ref, st_ref = rest
    else:
        o_ref, st_ref = rest

    @pl.when(pl.program_id(1) == 0)
    def _():
        st_ref[...] = jnp.zeros_like(st_ref)

    hw = N_HEAD * GLA_DK
    x = _dot(sm_ref[...].astype(BF16), w2_ref[...]) + b2_ref[...]
    gk_all = (jnp.minimum(x, 0.0) - jnp.log(1.0 + jnp.exp(-jnp.abs(x)))) * (1.0 / GLA_TAU)
    incl = incl_ref[...]
    id4 = id4_ref[...]
    order = list(_chunk_order(rev))
    units = [(ci, h) for ci in range(CPB) for h in range(N_HEAD)]
    rows = lambda c: slice(c * CHUNK, (c + 1) * CHUNK)
    kcols = lambda h: slice(h * GLA_DK, (h + 1) * GLA_DK)
    vcols = lambda h: slice(h * GLA_DV, (h + 1) * GLA_DV)

    gk = [gk_all[rows(c)] for c in order]
    q = [qk_ref[rows(c), :hw].astype(F32) * GLA_DK ** -0.5 for c in order]
    k = [qk_ref[rows(c), hw:].astype(F32) for c in order]
    vb = [v_ref[rows(c), :].astype(BF16) for c in order]
    G = [_dot_const(incl, g, 3) for g in gk]
    g_tot = [g[0:1] if rev else g[CHUNK - 1:CHUNK] for g in G]
    e1 = [_dot_const(pq_ref[0], g, 3) for g in gk]
    e2 = [_dot_const(pq_ref[1], g, 3) for g in gk]
    q_lvl = [[(qq * jnp.exp(e)).astype(BF16) for qq, e in zip(q, es)] for es in (e1, e2, gk)]
    q_lvl.append([qq.astype(BF16) for qq in q])
    q_in = [(qq * jnp.exp(g)).astype(BF16) for qq, g in zip(q, G)]
    k_out = [(kk * jnp.exp(gt - g)).astype(BF16) for kk, gt, g in zip(k, g_tot, G)]
    kb = [kk.astype(BF16) for kk in k]
    decay_tot = [jnp.exp(gt) for gt in g_tot]

    kt_all = [_dot_tn(t, id4) for t in kb]
    kt = [kt_all[ci][kcols(h)] for ci, h in units]
    sx = [_dot(q_lvl[3][ci][:, kcols(h)], t.astype(BF16)) * mx_ref[3] for (ci, h), t in zip(units, kt)]
    for l in range(3):
        e_all = [_dot_tn_const(g, kmt_ref[l], 2) for g in gk]
        e = [e_all[ci][kcols(h)] for ci, h in units]
        kt_l = [(t * jnp.exp(e_)).astype(BF16) for t, e_ in zip(kt, e)]
        sx = [s_ + _dot(q_lvl[l][ci][:, kcols(h)], t) * mx_ref[l] for (ci, h), s_, t in zip(units, sx, kt_l)]
    vh = [vb[ci][:, vcols(h)] for ci, h in units]
    o_intra = [_dot(s_.astype(BF16), jnp.concatenate([v_] * GLA_SUB, axis=0)) for s_, v_ in zip(sx, vh)]
    kv = [_dot_tn(v_, k_out[ci][:, kcols(h)]) for (ci, h), v_ in zip(units, vh)]

    st = [st_ref[h] for h in range(N_HEAD)]
    for ci in range(CPB):
        c = order[ci]
        for h in range(N_HEAD):
            n = ci * N_HEAD + h
            o = o_intra[n] + _dot_nt(q_in[ci][:, kcols(h)], st[h].astype(BF16))
            st[h] = st[h] * decay_tot[ci][:, kcols(h)] + kv[n]
            if finalize:
                o = _finalize(o, oprev_ref, gate_ref, nw_ref, c * CHUNK, h * GLA_DV, GLA_DV)
            o_ref[rows(c), vcols(h)] = o.astype(o_ref.dtype)
    for h in range(N_HEAD):
        st_ref[h] = st[h]


def _gla(p, p_small, w2, b2, norm_w, n_seq, n_lat_blk):
    out = None
    hw = N_HEAD * GLA_DK
    for rev in (True, False):
        finalize = not rev
        d = 1 if rev else 0
        lr0 = SM_LR + d * GLA_RANK
        w2p = jnp.zeros((V7X_LANES, hw), F32).at[lr0:lr0 + GLA_RANK].set(w2[d]).astype(BF16)
        rows = _scan_rows(n_seq, n_lat_blk, rev)
        extra = [(p_small, pl.BlockSpec((SCAN_BLK, V7X_LANES), lambda b_, j, rows=rows: (rows(b_, j), 0)))]
        kern = functools.partial(_gla_kernel, rev=rev, finalize=finalize)
        out = _scan_call(kern, "gla_bwd" if rev else "gla_fwd", n_seq, n_lat_blk, rev, finalize,
                         [(p, CB_GLA_QK), (p, CB_GLA_V)], [], [w2p, b2[d].reshape(1, hw), *_gla_consts(rev)],
                         extra, out, (p, CB_GLA_GATE), norm_w,
                         [pltpu.VMEM((N_HEAD, GLA_DV, GLA_DK), F32)])
    return out


def _hyena_filters(L, w1, b1, w2, b2, w3, freq):
    t = jnp.linspace(0.0, 1.0, L, dtype=F32)[:, None]
    w = 2.0 * math.pi * jnp.arange(L, dtype=F32) / L
    f = jnp.linspace(1e-4, HY_BANDS - 1, HY_BANDS, dtype=F32)
    ang = w[:, None] * f[None, :]
    z = jnp.concatenate([t, jnp.cos(ang), -jnp.sin(ang)], axis=-1)
    hdn = jnp.sin(freq[0] * (z @ w1 + b1))
    for i in range(HY_INNER):
        hdn = jnp.sin(freq[i + 1] * (hdn @ w2[i] + b2[i]))
    h = (hdn @ w3).reshape(L, HY_ORDER, 2, HY_W)
    deltas = jnp.abs(jnp.linspace(HY_MIN_DECAY, HY_MAX_DECAY, HY_W, dtype=F32))
    h = h * jnp.exp(-t[:, :, None, None] * deltas)
    hf, hb = h[:, :, 0], h[:, :, 1]
    hb = hb.at[0].set(0.0)
    scale = lax.rsqrt(jnp.sum(hf * hf, axis=0) + jnp.sum(hb * hb, axis=0) + EPS)
    return (hf * scale).reshape(L, -1), (hb * scale).reshape(L, -1)


def _dft_mats(L):
    k = jnp.arange(L, dtype=jnp.int32)[:, None]
    n = jnp.arange(L, dtype=jnp.int32)[None, :]
    ang = (((2 * k + 1) * n) % (4 * L)).astype(F32) * (math.pi / (2 * L))
    return jnp.stack([jnp.cos(ang), jnp.sin(ang)]).astype(BF16)


def _hy_spectra(hf, hb, dft):
    def mm2(m, g):
        n = g.shape[1]
        out = _matmul(m, jnp.concatenate(_split_bf16(g, 2), axis=1))
        return out[:, :n] + out[:, n:]

    return mm2(dft[0], hf + hb), mm2(dft[1], hb - hf)


def _hy_kernel(z_ref, x_ref, skip_ref, cs_ref, kr_ref, ki_ref, *rest, last, n_k, slab, inv_scale):
    if last:
        gate_ref, o_ref, acc_ref = rest
    else:
        o_ref, acc_ref = rest
    kt = pl.program_id(2)

    @pl.when(kt == 0)
    def _():
        acc_ref[...] = jnp.zeros_like(acc_ref)

    tk, L = cs_ref.shape[1], cs_ref.shape[2]
    cs = cs_ref[...].reshape(2 * tk, L)
    zf = _dot(cs, z_ref[...])
    zr, sz = zf[:tk], zf[tk:]
    kr, ki = kr_ref[...], ki_ref[...]
    pp = jnp.concatenate([zr * kr + sz * ki, sz * kr - zr * ki], axis=0).astype(BF16)
    for r0 in range(0, L, slab):
        acc_ref[r0:r0 + slab, :] += _dot_tn(cs[:, r0:r0 + slab], pp)

    @pl.when(kt == n_k - 1)
    def _():
        for r0 in range(0, acc_ref.shape[0], slab):
            conv = acc_ref[r0:r0 + slab, :] * inv_scale
            z = z_ref[r0:r0 + slab, :].astype(F32)
            y = x_ref[r0:r0 + slab, :].astype(F32) * (conv + skip_ref[...] * z)
            if last:
                y = y * _silu(gate_ref[r0:r0 + slab, :].astype(F32))
            o_ref[r0:r0 + slab, :] = y.astype(o_ref.dtype)


HY_WT = HY_W


def _hy_order(z_src, x_src, gate_src, skip, dft, kr, ki, order, n_seq, L):
    last = gate_src is not None
    tk = min(256, L)
    n_k = L // tk
    slab = min(512, L)
    n_wt = HY_W // HY_WT

    def rowspec(src):
        _, r0, c0 = src
        return pl.BlockSpec((L, HY_WT), lambda b, w, k: (r0 + b, c0 + w), pipeline_mode=pl.Buffered(1))

    srcs = [z_src, x_src] + ([gate_src] if last else [])
    twid = pl.BlockSpec((2, tk, L), lambda b, w, k: (0, k, 0))
    spec = pl.BlockSpec((tk, HY_WT), lambda b, w, k: (k, order * n_wt + w))
    kern = functools.partial(_hy_kernel, last=last, n_k=n_k, slab=slab, inv_scale=1.0 / L)
    args = [z_src[0], x_src[0], skip[order].reshape(1, HY_W), dft, kr, ki]
    specs = [rowspec(z_src), rowspec(x_src), pl.BlockSpec((1, HY_WT), lambda b, w, k: (0, w)),
             twid, spec, spec]
    if last:
        args.append(gate_src[0])
        specs.append(rowspec(gate_src))
    return pl.pallas_call(
        kern,
        grid=(n_seq, n_wt, n_k),
        in_specs=specs,
        out_specs=pl.BlockSpec((L, HY_WT), lambda b, w, k: (b, w)),
        out_shape=jax.ShapeDtypeStruct((n_seq * L, HY_W), BF16),
        scratch_shapes=[pltpu.VMEM((L, HY_WT), F32)],
        compiler_params=_cparams(("parallel", "parallel", "arbitrary")),
        name="hyena_order%d_L%d" % (order, L),
    )(*args)


def _hyena(p, conv_w, conv_b, w1, b1, w2, b2, w3, freq, skip, dfts, n_seq, seq_len):
    zcs = _conv_prep(p, CB_HY_V, conv_w, conv_b, n_seq, seq_len, False)
    n_wt = HY_W // HY_WT
    outs = []
    for zc, L, row0 in zip(zcs, (seq_len, CTX_LEN), (0, n_seq * seq_len // CTX_LEN)):
        dft = dfts[L]
        kr, ki = _hy_spectra(*_hyena_filters(L, w1, b1, w2, b2, w3, freq), dft)
        y1 = _hy_order((zc, 0, 0), (zc, 0, n_wt), None, skip, dft, kr, ki, 0, n_seq, L)
        outs.append(_hy_order((y1, 0, 0), (zc, 0, 2 * n_wt), (p, row0, CB_HY_GATE * n_wt),
                              skip, dft, kr, ki, 1, n_seq, L))
    return jnp.concatenate(outs, axis=0)


def _layout_w_in(w_in):
    sizes = [3 * HY_W, HY_W, 3 * BR_W, 2 * N_HEAD, 2 * N_HEAD, BR_W, 2 * N_HEAD * RET_DK, BR_W, BR_W,
             2 * N_HEAD * GLA_DK, BR_W, 2 * GLA_RANK, BR_W]
    offs = np.concatenate([[0], np.cumsum(sizes)])
    (hy_proj, hy_gate, gdn_qkv, gdn_a, gdn_b, gdn_gate, ret_qk, ret_v, ret_gate,
     gla_qk, gla_v, gla_lr, gla_gate) = [w_in[:, offs[n]:offs[n + 1]] for n in range(len(sizes))]
    D = w_in.shape[0]
    half = RET_DK // 2
    ret_rot = ret_qk.reshape(D, 2 * N_HEAD, 2, half)[:, :, ::-1].reshape(D, -1)
    main = jnp.concatenate([hy_proj, hy_gate, gdn_qkv, gdn_gate, ret_qk, ret_rot, ret_v, ret_gate,
                            gla_qk, gla_v, gla_gate], axis=1)
    small = jnp.concatenate([gdn_a, gdn_b, gla_lr], axis=1)
    small = jnp.pad(small, ((0, 0), (0, V7X_LANES - small.shape[1])))
    return main, small


def _layer(x, mod, dims, rope, dfts, norm_w, w_in, hy_conv_w, hy_conv_b, hy_w1, hy_b1, hy_w2, hy_b2, hy_w3,
           hy_freq, hy_skip, gdn_conv_w, gdn_a_log, gdn_dt_bias, gdn_norm, ret_norm, gla_w2, gla_b2,
           gla_norm, w_branch, w_merge, b_merge, w_out):
    n_seq, seq_len = dims
    n_lat_rows = n_seq * seq_len
    n_lat_blk = seq_len // SCAN_BLK
    shift, scale, gate = mod
    w_main, w_small = _layout_w_in(w_in.astype(BF16))
    p, h = _inproj(x, norm_w, scale, shift, w_main, n_lat_rows, seq_len)
    p_small = _matmul(h, w_small)

    y_hy = _hyena(p, hy_conv_w, hy_conv_b, hy_w1, hy_b1, hy_w2, hy_b2, hy_w3, hy_freq, hy_skip, dfts,
                  n_seq, seq_len)
    y_gdn = _gdn(p, p_small, gdn_conv_w, gdn_a_log, gdn_dt_bias, gdn_norm, n_seq, n_lat_blk)
    y_ret = _retention(p, rope[0], rope[1], ret_norm, n_seq, n_lat_blk)
    y_gla = _gla(p, p_small, gla_w2, gla_b2, gla_norm, n_seq, n_lat_blk)

    merged = _merge(h, (y_hy, y_gdn, y_ret, y_gla), w_merge.astype(BF16), b_merge, w_branch.astype(BF16))
    return _outproj(merged, w_out.astype(BF16), x, gate, n_lat_rows, seq_len)


def kernel(x, c, ctx, c_ctx, norm_w, ada_w, ada_b, w_in, hy_conv_w, hy_conv_b, hy_w1, hy_b1,
           hy_w2, hy_b2, hy_w3, hy_freq, hy_skip, gdn_conv_w, gdn_a_log, gdn_dt_bias,
           gdn_norm, ret_norm, gla_w2, gla_b2, gla_norm, w_branch, w_merge, b_merge, w_out,
           final_norm):
    B, S, D = x.shape
    layer_params = (norm_w, w_in, hy_conv_w, hy_conv_b, hy_w1, hy_b1, hy_w2, hy_b2, hy_w3, hy_freq,
                    hy_skip, gdn_conv_w, gdn_a_log, gdn_dt_bias, gdn_norm, ret_norm, gla_w2, gla_b2,
                    gla_norm, w_branch, w_merge, b_merge, w_out)
    xt = jnp.concatenate([x.reshape(-1, D), ctx.reshape(-1, D)], axis=0)
    cond = _silu(jnp.concatenate([c_ctx[None], c], axis=0))
    cond = jnp.pad(cond, ((0, -(1 + B) % 8), (0, 0))).astype(BF16)
    rope = _rope_tables(S)
    dfts = {L: _dft_mats(L) for L in (S, CTX_LEN)}
    for l in range(DEPTH):
        m = (_matmul(cond, ada_w[l].astype(BF16)) + ada_b[l])[:1 + B]
        mod = tuple(t.reshape(1 + B, 1, D) for t in jnp.split(m, 3, axis=-1))
        xt = _layer(xt, mod, (B, S), rope, dfts, *[p[l] for p in layer_params])
    return _final_norm(xt, final_norm, B * S).reshape(B, S, D)
```

```python
import functools
import math

import numpy as np
import jax
import jax.numpy as jnp
from jax import lax
from jax.experimental import pallas as pl
from jax.experimental.pallas import tpu as pltpu

D_MODEL = 2048
DEPTH = 4
CTX_LEN = 256
GRID_W = 64
F32 = jnp.float32
BF16 = jnp.bfloat16
EPS = 1e-6
CHUNK = 64

N_BRANCH = 4
BR_W = D_MODEL // N_BRANCH

HY_W = BR_W
HY_ORDER = 2
HY_EMB = 33
HY_BANDS = (HY_EMB - 1) // 2
HY_INNER = 2
HY_FAST_DECAY = 0.3
HY_SLOW_DECAY = 1.5
HY_TARGET = 1e-2
HY_MIN_DECAY = math.log(HY_TARGET) / HY_SLOW_DECAY
HY_MAX_DECAY = math.log(HY_TARGET) / HY_FAST_DECAY

N_HEAD = 4
GDN_DK = BR_W // N_HEAD
GDN_DV = BR_W // N_HEAD
RET_DK = BR_W // (2 * N_HEAD)
RET_DV = BR_W // N_HEAD
RET_ROPE_PAIRS = (8, 12, 12)
ROPE_BASE = 10000.0
GLA_DK = BR_W // (2 * N_HEAD)
GLA_DV = BR_W // N_HEAD
GLA_RANK = 16
GLA_TAU = 16.0

V7X_LANES = 128
VMEM_LIMIT_BYTES = 56 * 1024 * 1024

SCAN_BLK = CTX_LEN
CPB = SCAN_BLK // CHUNK
COLW = BR_W

CB_HY_V, CB_HY_X1, CB_HY_X2, CB_HY_GATE = 0, 1, 2, 3
CB_GDN_Q, CB_GDN_GATE = 4, 7
CB_RET_QK, CB_RET_QKROT, CB_RET_V, CB_RET_GATE = 8, 9, 10, 11
CB_GLA_QK, CB_GLA_V, CB_GLA_GATE = 12, 13, 14
N_COLBLK = 15
SM_A, SM_B, SM_LR = 0, 2 * N_HEAD, 4 * N_HEAD


def _dot(a, b):
    return jnp.dot(a, b, preferred_element_type=F32)


def _dot_nt(a, b):
    return lax.dot_general(a, b, (((1,), (1,)), ((), ())), preferred_element_type=F32)


def _dot_tn(a, b):
    return lax.dot_general(a, b, (((0,), (0,)), ((), ())), preferred_element_type=F32)


def _split_bf16(x, terms):
    parts = []
    r = x
    for t in range(terms):
        p = r.astype(BF16)
        parts.append(p)
        if t + 1 < terms:
            r = r - p.astype(F32)
    return parts


def _dot_f32(a, b):
    a1, a2 = _split_bf16(a, 2)
    b1, b2 = _split_bf16(b, 2)
    return _dot(a1, b1) + (_dot(a1, b2) + _dot(a2, b1))


def _dot_const(c, x, terms):
    out = None
    for p in _split_bf16(x, terms):
        t = _dot(c, p)
        out = t if out is None else out + t
    return out


def _dot_tn_const(x, c, terms):
    out = None
    for p in _split_bf16(x, terms):
        t = _dot_tn(p, c)
        out = t if out is None else out + t
    return out


def _silu(x):
    return x * jax.nn.sigmoid(x)


def _cparams(sem):
    return pltpu.CompilerParams(dimension_semantics=sem, vmem_limit_bytes=VMEM_LIMIT_BYTES)


def _mm_kernel(a_ref, b_ref, o_ref):
    o_ref[...] = _dot(a_ref[...], b_ref[...]).astype(o_ref.dtype)


def _pick_tile(n, cands):
    for c in cands:
        if n % c == 0:
            return c
    return n


def _matmul(a, b, out_dtype=F32):
    M, K = a.shape
    _, N = b.shape
    tm = _pick_tile(M, (1024, 512, 256, 128, 64, 32, 16, 8))
    tn = _pick_tile(N, (512, 384, 256, 128))
    return pl.pallas_call(
        _mm_kernel,
        grid=(M // tm, N // tn),
        in_specs=[pl.BlockSpec((tm, K), lambda i, j: (i, 0)),
                  pl.BlockSpec((K, tn), lambda i, j: (0, j))],
        out_specs=pl.BlockSpec((tm, tn), lambda i, j: (i, j)),
        out_shape=jax.ShapeDtypeStruct((M, N), out_dtype),
        compiler_params=_cparams(("parallel", "arbitrary")),
        name="matmul",
    )(a, b)


def _mod_row(n_lat_blk, blk_per_seq):
    return lambda i: jnp.where(i < n_lat_blk, 1 + i // blk_per_seq, 0)


def _inproj_kernel(x_ref, nw_ref, sc_ref, sh_ref, w_ref, p_ref, h_ref, hs_ref):
    @pl.when(pl.program_id(1) == 0)
    def _():
        for r0 in range(0, x_ref.shape[0], PREP_SLAB):
            x = x_ref[r0:r0 + PREP_SLAB, :]
            y = x * lax.rsqrt(jnp.mean(x * x, axis=-1, keepdims=True) + EPS) * nw_ref[...]
            h = (y * (1.0 + sc_ref[0]) + sh_ref[0]).astype(BF16)
            hs_ref[r0:r0 + PREP_SLAB, :] = h
            h_ref[r0:r0 + PREP_SLAB, :] = h

    p_ref[...] = _dot(hs_ref[...], w_ref[...]).astype(p_ref.dtype)


def _inproj(x, norm_w, scale, shift, w, n_lat_rows, seq_len):
    T, D = x.shape
    N = w.shape[1]
    tm, tn = 1024, 3 * COLW
    row = _mod_row(n_lat_rows // tm, seq_len // tm)
    return pl.pallas_call(
        _inproj_kernel,
        grid=(T // tm, N // tn),
        in_specs=[pl.BlockSpec((tm, D), lambda i, j: (i, 0)),
                  pl.BlockSpec((1, D), lambda i, j: (0, 0)),
                  pl.BlockSpec((1, 1, D), lambda i, j: (row(i), 0, 0)),
                  pl.BlockSpec((1, 1, D), lambda i, j: (row(i), 0, 0)),
                  pl.BlockSpec((D, tn), lambda i, j: (0, j))],
        out_specs=[pl.BlockSpec((tm, tn), lambda i, j: (i, j)),
                   pl.BlockSpec((tm, D), lambda i, j: (i, 0))],
        out_shape=[jax.ShapeDtypeStruct((T, N), BF16), jax.ShapeDtypeStruct((T, D), BF16)],
        scratch_shapes=[pltpu.VMEM((tm, D), BF16)],
        compiler_params=_cparams(("parallel", "arbitrary")),
        name="inproj",
    )(x, norm_w.reshape(1, D), scale, shift, w)


def _merge_kernel(h_ref, y0_ref, y1_ref, y2_ref, y3_ref, wm_ref, bm_ref, wb_ref, o_ref):
    h = h_ref[...]
    acc = None
    for i, y_ref in enumerate((y0_ref, y1_ref, y2_ref, y3_ref)):
        g = jax.nn.sigmoid(_dot(h, wm_ref[i]) + bm_ref[i])
        t = g * _dot(y_ref[...], wb_ref[i])
        acc = t if acc is None else acc + t
    o_ref[...] = acc.astype(o_ref.dtype)


def _merge(h, ys, w_merge, b_merge, w_branch):
    T, D = h.shape
    tm, tn = 1024, 512
    yspec = pl.BlockSpec((tm, BR_W), lambda i, j: (i, 0))
    return pl.pallas_call(
        _merge_kernel,
        grid=(T // tm, D // tn),
        in_specs=[pl.BlockSpec((tm, D), lambda i, j: (i, 0)), yspec, yspec, yspec, yspec,
                  pl.BlockSpec((N_BRANCH, D, tn), lambda i, j: (0, 0, j)),
                  pl.BlockSpec((N_BRANCH, 1, tn), lambda i, j: (0, 0, j)),
                  pl.BlockSpec((N_BRANCH, BR_W, tn), lambda i, j: (0, 0, j))],
        out_specs=pl.BlockSpec((tm, tn), lambda i, j: (i, j)),
        out_shape=jax.ShapeDtypeStruct((T, D), BF16),
        compiler_params=_cparams(("parallel", "arbitrary")),
        name="merge",
    )(h, *ys, w_merge, b_merge.reshape(N_BRANCH, 1, D), w_branch)


def _outproj_kernel(a_ref, w_ref, x_ref, g_ref, o_ref):
    o_ref[...] = x_ref[...] + g_ref[0] * _dot(a_ref[...], w_ref[...])


def _outproj(a, w, x, gate, n_lat_rows, seq_len):
    T, D = x.shape
    tm, tn = 1024, 1024
    row = _mod_row(n_lat_rows // tm, seq_len // tm)
    return pl.pallas_call(
        _outproj_kernel,
        grid=(T // tm, D // tn),
        in_specs=[pl.BlockSpec((tm, D), lambda i, j: (i, 0)),
                  pl.BlockSpec((D, tn), lambda i, j: (0, j)),
                  pl.BlockSpec((tm, tn), lambda i, j: (i, j)),
                  pl.BlockSpec((1, 1, tn), lambda i, j: (row(i), 0, j))],
        out_specs=pl.BlockSpec((tm, tn), lambda i, j: (i, j)),
        out_shape=jax.ShapeDtypeStruct((T, D), F32),
        compiler_params=_cparams(("parallel", "arbitrary")),
        name="outproj",
    )(a, w, x, gate)


def _final_norm_kernel(x_ref, w_ref, o_ref):
    x = x_ref[...]
    o_ref[...] = x * lax.rsqrt(jnp.mean(x * x, axis=-1, keepdims=True) + EPS) * w_ref[...]


def _final_norm(x, w, n_lat_rows):
    D = x.shape[1]
    tm = 512
    return pl.pallas_call(
        _final_norm_kernel,
        grid=(n_lat_rows // tm,),
        in_specs=[pl.BlockSpec((tm, D), lambda i: (i, 0)),
                  pl.BlockSpec((1, D), lambda i: (0, 0))],
        out_specs=pl.BlockSpec((tm, D), lambda i: (i, 0)),
        out_shape=jax.ShapeDtypeStruct((n_lat_rows, D), F32),
        compiler_params=_cparams(("parallel",)),
        name="final_norm",
    )(x, w.reshape(1, D))


def _scan_rows(n_seq, n_lat_blk, rev):
    def idx(b, j):
        jj = (n_lat_blk - j) if rev else (j - 1)
        return jnp.where(j == 0, n_seq * n_lat_blk + b, b * n_lat_blk + jj)
    return idx


def _chunk_order(rev):
    return range(CPB - 1, -1, -1) if rev else range(CPB)


def _before(rev):
    i = np.arange(CHUNK)[:, None]
    j = np.arange(CHUNK)[None, :]
    return (j > i) if rev else (j < i)


def _finalize(o, oprev_ref, gate_ref, nw_ref, r0, c0, width):
    o = o + oprev_ref[r0:r0 + CHUNK, c0:c0 + width]
    o = o * lax.rsqrt(jnp.mean(o * o, axis=-1, keepdims=True) + EPS) * nw_ref[...]
    return o * _silu(gate_ref[r0:r0 + CHUNK, c0:c0 + width].astype(F32))


def _scan_call(kernel, name, n_seq, n_lat_blk, rev, finalize, row_inputs, tab_inputs, const_inputs,
               extra_specs_inputs, oprev, gate_src, norm_w, scratch):
    rows = _scan_rows(n_seq, n_lat_blk, rev)
    T = row_inputs[0][0].shape[0]
    args, specs = [], []
    for arr, cb in row_inputs:
        args.append(arr)
        specs.append(pl.BlockSpec((SCAN_BLK, COLW), functools.partial(
            lambda b, j, cb: (rows(b, j), cb), cb=cb)))
    for arr in tab_inputs:
        args.append(arr)
        specs.append(pl.BlockSpec((SCAN_BLK, arr.shape[1]), lambda b, j: (
            jnp.where(j == 0, 0, 1 + ((n_lat_blk - j) if rev else (j - 1))), 0)))
    for arr, spec in extra_specs_inputs:
        args.append(arr)
        specs.append(spec)
    for arr in const_inputs:
        args.append(arr)
        specs.append(pl.BlockSpec(arr.shape, functools.partial(lambda b, j, n: (0,) * n, n=arr.ndim)))
    if finalize:
        args += [oprev, gate_src[0], norm_w.reshape(1, -1)]
        specs += [pl.BlockSpec((SCAN_BLK, COLW), lambda b, j: (rows(b, j), 0)),
                  pl.BlockSpec((SCAN_BLK, COLW), functools.partial(
                      lambda b, j, cb: (rows(b, j), cb), cb=gate_src[1])),
                  pl.BlockSpec((1, norm_w.shape[0]), lambda b, j: (0, 0))]
    return pl.pallas_call(
        kernel,
        grid=(n_seq, 1 + n_lat_blk),
        in_specs=specs,
        out_specs=pl.BlockSpec((SCAN_BLK, COLW), lambda b, j: (rows(b, j), 0)),
        out_shape=jax.ShapeDtypeStruct((T, COLW), BF16 if finalize else F32),
        scratch_shapes=scratch,
        compiler_params=_cparams(("parallel", "arbitrary")),
        name=name,
    )(*args)


def _ret_consts(rev):
    lg = np.log(1.0 - np.power(2.0, -5.0 - np.arange(N_HEAD, dtype=np.float64)))
    i = np.arange(CHUNK, dtype=np.float64)
    steps = (CHUNK - i) if rev else (i + 1.0)
    G = steps[None, :] * lg[:, None]
    mask = _before(rev) | np.eye(CHUNK, dtype=bool)
    dm = np.where(mask[None], np.exp(G[:, :, None] - G[:, None, :]), 0.0)
    qs = np.repeat(np.exp(G).T, RET_DK, axis=1)
    ks = np.repeat(np.exp(CHUNK * lg[:, None] - G).T, RET_DK, axis=1)
    a_end = tuple(float(v) for v in np.exp(CHUNK * lg))
    return (jnp.asarray(dm, F32), jnp.asarray(qs, F32), jnp.asarray(ks, F32)), a_end


def _ret_kernel(qk_ref, qkr_ref, v_ref, cos_ref, sin_ref, dm_ref, qs_ref, ks_ref, *rest,
                rev, finalize, a_end):
    if finalize:
        oprev_ref, gate_ref, nw_ref, o_ref, s_ref = rest
    else:
        o_ref, s_ref = rest

    @pl.when(pl.program_id(1) == 0)
    def _():
        s_ref[...] = jnp.zeros_like(s_ref)

    hw = N_HEAD * RET_DK
    qk = (qk_ref[...].astype(F32) * cos_ref[...]
          + qkr_ref[...].astype(F32) * sin_ref[...])
    order = list(_chunk_order(rev))
    units = [(ci, h) for ci in range(CPB) for h in range(N_HEAD)]
    rows = lambda c: slice(c * CHUNK, (c + 1) * CHUNK)
    kcols = lambda h: slice(h * RET_DK, (h + 1) * RET_DK)
    vcols = lambda h: slice(h * RET_DV, (h + 1) * RET_DV)

    qc = [qk[rows(c), :hw] for c in order]
    kc = [qk[rows(c), hw:] for c in order]
    qb = [t.astype(BF16) for t in qc]
    kb = [t.astype(BF16) for t in kc]
    q_in = [(t * qs_ref[...]).astype(BF16) for t in qc]
    k_out = [(t * ks_ref[...]).astype(BF16) for t in kc]
    vb = [v_ref[rows(c), :].astype(BF16) for c in order]
    vh = [vb[ci][:, vcols(h)] for ci, h in units]
    sc = [(_dot_nt(qb[ci][:, kcols(h)], kb[ci][:, kcols(h)]) * dm_ref[h]).astype(BF16) for ci, h in units]
    o_intra = [_dot(s_, v_) for s_, v_ in zip(sc, vh)]
    kv = [_dot_tn(k_out[ci][:, kcols(h)], v_) for (ci, h), v_ in zip(units, vh)]

    s = [s_ref[h] for h in range(N_HEAD)]
    for ci in range(CPB):
        c = order[ci]
        for h in range(N_HEAD):
            n = ci * N_HEAD + h
            o = o_intra[n] + _dot(q_in[ci][:, kcols(h)], s[h].astype(BF16))
            s[h] = a_end[h] * s[h] + kv[n]
            if finalize:
                o = _finalize(o, oprev_ref, gate_ref, nw_ref, c * CHUNK, h * RET_DV, RET_DV)
            o_ref[rows(c), vcols(h)] = o.astype(o_ref.dtype)
    for h in range(N_HEAD):
        s_ref[h] = s[h]


def _retention(p, cos_tab, sin_tab, norm_w, n_seq, n_lat_blk):
    out = None
    for rev in (True, False):
        finalize = not rev
        consts, a_end = _ret_consts(rev)
        kern = functools.partial(_ret_kernel, rev=rev, finalize=finalize, a_end=a_end)
        out = _scan_call(kern, "retention_bwd" if rev else "retention_fwd", n_seq, n_lat_blk, rev, finalize,
                         [(p, CB_RET_QK), (p, CB_RET_QKROT), (p, CB_RET_V)], [cos_tab, sin_tab],
                         list(consts), [], out, (p, CB_RET_GATE), norm_w,
                         [pltpu.VMEM((N_HEAD, RET_DK, RET_DV), F32)])
    return out


def _rope_tables(seq_len):
    def angles(pos):
        angs = []
        for a, n in enumerate(RET_ROPE_PAIRS):
            freqs = ROPE_BASE ** (-jnp.arange(n, dtype=F32) / n)
            angs.append(pos[:, a:a + 1] * freqs)
        return jnp.concatenate(angs, axis=-1)

    t = jnp.arange(CTX_LEN, dtype=F32)
    zero = jnp.zeros((CTX_LEN,), F32)
    pos_c = jnp.stack([t, zero, zero], axis=-1)
    rows = seq_len // GRID_W
    r = jnp.repeat(jnp.arange(rows, dtype=F32), GRID_W)
    col = jnp.broadcast_to(jnp.arange(GRID_W, dtype=F32), (rows, GRID_W)).reshape(-1)
    pos_l = jnp.stack([jnp.full((seq_len,), CTX_LEN, F32), r, col], axis=-1)
    ang = jnp.concatenate([angles(pos_c), angles(pos_l)], axis=0)
    cos = jnp.concatenate([jnp.cos(ang), jnp.cos(ang)], axis=-1)
    sin = jnp.concatenate([-jnp.sin(ang), jnp.sin(ang)], axis=-1)
    scale = jnp.concatenate([jnp.ones((N_HEAD * RET_DK,), F32),
                             jnp.full((N_HEAD * RET_DK,), RET_DK ** -0.5, F32)])
    return jnp.tile(cos, (1, 2 * N_HEAD)) * scale, jnp.tile(sin, (1, 2 * N_HEAD)) * scale


def _gdn_consts(rev):
    i = np.arange(CHUNK)
    before = _before(rev)
    eye = np.eye(CHUNK, dtype=bool)
    same16 = (i[:, None] // 16) == (i[None, :] // 16)
    same32 = (i[:, None] // 32) == (i[None, :] // 32)
    masks = np.stack([before | eye, before, before & same16, before & same32 & ~same16,
                      before & ~same32, eye]).astype(np.float32)
    return jnp.asarray(masks)


def _unit_tri_inverse(a_all, m_blk, m_mid, m_top, eye):
    both = lambda f, xs, ys: [f(x, y) for x, y in zip(xs, ys)]
    a_blk = [a * m_blk for a in a_all]
    x = [eye + a for a in a_blk]
    pw = both(_dot_f32, a_blk, a_blk)
    for _ in range(2):
        x = both(lambda x_, p_: x_ + _dot_f32(x_, p_), x, pw)
        pw = both(_dot_f32, pw, pw)
    x = both(lambda x_, p_: x_ + _dot_f32(x_, p_), x, pw)
    for m in (m_mid, m_top):
        y = both(lambda x_, a_: _dot_f32(x_, a_ * m), x, a_all)
        x = both(lambda x_, y_: x_ + _dot_f32(y_, x_), x, y)
    return x


def _gdn_kernel(q_ref, k_ref, v_ref, sm_ref, ea_ref, dtb_ref, sel_ref, m_ref, *rest, rev, finalize):
    if finalize:
        oprev_ref, gate_ref, nw_ref, o_ref, s_ref = rest
    else:
        o_ref, s_ref = rest

    @pl.when(pl.program_id(1) == 0)
    def _():
        s_ref[...] = jnp.zeros_like(s_ref)

    m_incl, m_strict, m_blk, m_mid, m_top, eye = (m_ref[n] for n in range(6))
    order = list(_chunk_order(rev))
    units = [(c, h) for c in order for h in range(N_HEAD)]
    rows = lambda c: slice(c * CHUNK, (c + 1) * CHUNK)
    cols = lambda h: slice(h * GDN_DK, (h + 1) * GDN_DK)
    d = 1 if rev else 0
    lane_a = lambda h: slice(SM_A + d * N_HEAD + h, SM_A + d * N_HEAD + h + 1)
    lane_b = lambda h: slice(SM_B + d * N_HEAD + h, SM_B + d * N_HEAD + h + 1)

    x = sm_ref[...]
    xa = x + dtb_ref[...]
    g_all = -ea_ref[...] * (jnp.maximum(xa, 0.0) + jnp.log(1.0 + jnp.exp(-jnp.abs(xa))))
    beta_all = jax.nn.sigmoid(x)
    incl16 = m_incl.astype(BF16)
    G = {c: _dot_const(incl16, g_all[rows(c)], 3) for c in order}
    G_rows = {c: sum(_dot_nt(sel_ref[...], p_) for p_ in _split_bf16(G[c], 3)) for c in order}

    k16 = [k_ref[rows(c), cols(h)] for c, h in units]
    q = [q_ref[rows(c), cols(h)].astype(F32) for c, h in units]
    k = [t.astype(F32) for t in k16]
    v = [v_ref[rows(c), cols(h)].astype(F32) for c, h in units]
    g_col = [G[c][:, lane_a(h)] for c, h in units]
    beta = [beta_all[rows(c), lane_b(h)] for c, h in units]
    g_row = [G_rows[c][h:h + 1, :] for c, h in units]
    g_tot = [g[0:1] if rev else g[CHUNK - 1:CHUNK] for g in g_col]
    decay = [jnp.exp(jnp.where(m_incl > 0, gc - gr, -jnp.inf)) for gc, gr in zip(g_col, g_row)]
    k_beta = [t * b for t, b in zip(k, beta)]
    a = [-(_dot_nt(kb.astype(BF16), kk) * d) * m_strict for kb, kk, d in zip(k_beta, k16, decay)]
    t16 = [(x - eye).astype(BF16) for x in _unit_tri_inverse(a, m_blk, m_mid, m_top, eye)]
    v_beta = [t * b for t, b in zip(v, beta)]
    e_col = [jnp.exp(g) for g in g_col]
    kbg = [kb * e for kb, e in zip(k_beta, e_col)]
    u = [vb + _dot(t, vb.astype(BF16)) for t, vb in zip(t16, v_beta)]
    w16 = [(x + _dot(t, x.astype(BF16))).astype(BF16) for t, x in zip(t16, kbg)]
    sc16 = [(_dot_nt(qq.astype(BF16), kk) * d).astype(BF16) for qq, kk, d in zip(q, k16, decay)]
    q_in = [(qq * e).astype(BF16) for qq, e in zip(q, e_col)]
    k_out = [(kk * jnp.exp(gt - gc)).astype(BF16) for kk, gt, gc in zip(k, g_tot, g_col)]
    a_end = [jnp.exp(gt) for gt in g_tot]

    s = [s_ref[h] for h in range(N_HEAD)]
    for ci in range(CPB):
        us = range(ci * N_HEAD, (ci + 1) * N_HEAD)
        s16 = [t.astype(BF16) for t in s]
        vn16 = [(u[n] - _dot(w16[n], s16[h])).astype(BF16) for h, n in enumerate(us)]
        o = [_dot(q_in[n], s16[h]) + _dot(sc16[n], vn16[h]) for h, n in enumerate(us)]
        s = [a_end[n] * s[h] + _dot_tn(k_out[n], vn16[h]) for h, n in enumerate(us)]
        for h, n in enumerate(us):
            c = units[n][0]
            oh = o[h]
            if finalize:
                oh = _finalize(oh, oprev_ref, gate_ref, nw_ref, c * CHUNK, h * GDN_DV, GDN_DV)
            o_ref[rows(c), cols(h)] = oh.astype(o_ref.dtype)
    for h in range(N_HEAD):
        s_ref[h] = s[h]


PREP_SLAB = 512


def _conv_prep_kernel(z_ref, w_ref, b_ref, o_ref, *, gdn):
    L = z_ref.shape[0]
    slab = min(PREP_SLAB, L)
    cb = pl.program_id(1)
    w0, w1, w2 = w_ref[0:1, :], w_ref[1:2, :], w_ref[2:3, :]
    row = lax.broadcasted_iota(jnp.int32, (slab, z_ref.shape[1]), 0)
    zero_row = jnp.zeros((1, z_ref.shape[1]), F32)
    for r0 in range(0, L, slab):
        z = z_ref[r0:r0 + slab, :].astype(F32)
        halo = 16
        before = z_ref[r0 - halo:r0, :].astype(F32)[halo - 1:halo] if r0 > 0 else zero_row
        after = z_ref[r0 + slab:r0 + slab + halo, :].astype(F32)[0:1] if r0 + slab < L else zero_row
        prev = jnp.where(row == 0, before, pltpu.roll(z, 1, 0))
        nxt = jnp.where(row == slab - 1, after, pltpu.roll(z, slab - 1, 0))
        y = prev * w0 + z * w1 + nxt * w2
        if not gdn:
            o_ref[r0:r0 + slab, :] = (y + b_ref[...]).astype(o_ref.dtype)
            continue
        y = _silu(y)

        @pl.when(cb < 2)
        def _():
            scale = jnp.where(cb == 0, GDN_DK ** -0.5, 1.0)
            for h in range(N_HEAD):
                t = y[:, h * GDN_DK:(h + 1) * GDN_DK]
                t = t * (lax.rsqrt(jnp.sum(t * t, axis=-1, keepdims=True) + EPS) * scale)
                o_ref[r0:r0 + slab, h * GDN_DK:(h + 1) * GDN_DK] = t.astype(o_ref.dtype)

        @pl.when(cb == 2)
        def _():
            o_ref[r0:r0 + slab, :] = y.astype(o_ref.dtype)


def _conv_prep(p, col0, conv_w, conv_b, n_seq, seq_len, gdn):
    bias = jnp.zeros((1, 3 * COLW), F32) if conv_b is None else conv_b.reshape(1, -1)
    outs = []
    for L, row0 in ((seq_len, 0), (CTX_LEN, n_seq * seq_len // CTX_LEN)):
        outs.append(pl.pallas_call(
            functools.partial(_conv_prep_kernel, gdn=gdn),
            grid=(n_seq, 3),
            in_specs=[pl.BlockSpec((L, COLW), lambda b, c, row0=row0: (row0 + b, col0 + c)),
                      pl.BlockSpec((3, COLW), lambda b, c: (0, c)),
                      pl.BlockSpec((1, COLW), lambda b, c: (0, c))],
            out_specs=pl.BlockSpec((L, COLW), lambda b, c: (b, c)),
            out_shape=jax.ShapeDtypeStruct((n_seq * L, 3 * COLW), BF16),
            compiler_params=_cparams(("parallel", "parallel")),
            name="conv_prep_%s_L%d" % ("gdn" if gdn else "hyena", L),
        )(p, conv_w, bias))
    return outs


def _gdn(p, p_small, conv_w, a_log, dt_bias, norm_w, n_seq, n_lat_blk):
    T = p.shape[0]
    qkv = jnp.concatenate(_conv_prep(p, CB_GDN_Q, conv_w, None, n_seq, n_lat_blk * SCAN_BLK, True),
                          axis=0)
    pad = lambda t: jnp.pad(t.reshape(1, -1), ((0, 0), (SM_A, V7X_LANES - SM_A - 2 * N_HEAD)))
    exp_a, dtb = pad(jnp.exp(a_log)), pad(dt_bias)
    out = None
    for rev in (True, False):
        finalize = not rev
        d = 1 if rev else 0
        sel = np.zeros((16, V7X_LANES), np.float32)
        sel[np.arange(N_HEAD), SM_A + d * N_HEAD + np.arange(N_HEAD)] = 1.0
        rows = _scan_rows(n_seq, n_lat_blk, rev)
        extra = [(p_small, pl.BlockSpec((SCAN_BLK, V7X_LANES), lambda b_, j, rows=rows: (rows(b_, j), 0)))]
        kern = functools.partial(_gdn_kernel, rev=rev, finalize=finalize)
        out = _scan_call(kern, "gdn_bwd" if rev else "gdn_fwd", n_seq, n_lat_blk, rev, finalize,
                         [(qkv, 0), (qkv, 1), (qkv, 2)], [],
                         [exp_a, dtb, jnp.asarray(sel, BF16), _gdn_consts(rev)], extra,
                         out, (p, CB_GDN_GATE), norm_w,
                         [pltpu.VMEM((N_HEAD, GDN_DK, GDN_DV), F32)])
    return out


GLA_SUB = 4


def _gla_consts(rev):
    C = CHUNK
    idx = np.arange(C)
    before = _before(rev)
    incl = before | np.eye(C, dtype=bool)
    sizes = (16, 4, 1)
    blk = [idx // s for s in sizes]
    par = [idx // 64, idx // 16, idx // 4]
    sub = [b % GLA_SUB for b in blk]

    def blk_before(l, a, b):
        return (blk[l][a] > blk[l][b]) if rev else (blk[l][a] < blk[l][b])

    pq = np.zeros((2, C, C), np.float32)
    for l in range(2):
        pq[l] = incl & (blk[l][:, None] == blk[l][None, :])
    kmt = np.zeros((3, C, GLA_SUB * C), np.float32)
    maskx = np.zeros((4, C, GLA_SUB * C), np.float32)
    for l in range(3):
        for m in range(GLA_SUB):
            for j in range(C):
                tgt = par[l][j] * GLA_SUB + m
                valid = (tgt < blk[l][j]) if rev else (tgt > blk[l][j])
                if not valid:
                    continue
                t_after_j = before[:, j]
                t_blk_before_tgt = (blk[l] > tgt) if rev else (blk[l] < tgt)
                kmt[l, :, m * C + j] = t_after_j & t_blk_before_tgt
            for i in range(C):
                if sub[l][i] != m:
                    continue
                sel = (par[l] == par[l][i]) & np.array([blk_before(l, j, i) for j in range(C)])
                maskx[l, i, m * C:(m + 1) * C] = sel
    maskx[3, :, :C] = np.eye(C)
    id4 = np.tile(np.eye(C, dtype=np.float32), (1, GLA_SUB))
    return (jnp.asarray(incl.astype(np.float32), BF16), jnp.asarray(pq, BF16), jnp.asarray(kmt, BF16),
            jnp.asarray(maskx, F32), jnp.asarray(id4, BF16))


def _gla_kernel(qk_ref, v_ref, sm_ref, w2_ref, b2_ref, incl_ref, pq_ref, kmt_ref, mx_ref, id4_ref, *rest,
                rev, finalize):
    if finalize:
        oprev_ref, gate_ref, nw_ref, o_ref, st_ref = rest
    else:
        o_ref, st_ref = rest

    @pl.when(pl.program_id(1) == 0)
    def _():
        st_ref[...] = jnp.zeros_like(st_ref)

    hw = N_HEAD * GLA_DK
    x = _dot(sm_ref[...].astype(BF16), w2_ref[...]) + b2_ref[...]
    gk_all = (jnp.minimum(x, 0.0) - jnp.log(1.0 + jnp.exp(-jnp.abs(x)))) * (1.0 / GLA_TAU)
    incl = incl_ref[...]
    id4 = id4_ref[...]
    order = list(_chunk_order(rev))
    units = [(ci, h) for ci in range(CPB) for h in range(N_HEAD)]
    rows = lambda c: slice(c * CHUNK, (c + 1) * CHUNK)
    kcols = lambda h: slice(h * GLA_DK, (h + 1) * GLA_DK)
    vcols = lambda h: slice(h * GLA_DV, (h + 1) * GLA_DV)

    gk = [gk_all[rows(c)] for c in order]
    q = [qk_ref[rows(c), :hw].astype(F32) * GLA_DK ** -0.5 for c in order]
    k = [qk_ref[rows(c), hw:].astype(F32) for c in order]
    vb = [v_ref[rows(c), :].astype(BF16) for c in order]
    G = [_dot_const(incl, g, 3) for g in gk]
    g_tot = [g[0:1] if rev else g[CHUNK - 1:CHUNK] for g in G]
    e1 = [_dot_const(pq_ref[0], g, 3) for g in gk]
    e2 = [_dot_const(pq_ref[1], g, 3) for g in gk]
    q_lvl = [[(qq * jnp.exp(e)).astype(BF16) for qq, e in zip(q, es)] for es in (e1, e2, gk)]
    q_lvl.append([qq.astype(BF16) for qq in q])
    q_in = [(qq * jnp.exp(g)).astype(BF16) for qq, g in zip(q, G)]
    k_out = [(kk * jnp.exp(gt - g)).astype(BF16) for kk, gt, g in zip(k, g_tot, G)]
    kb = [kk.astype(BF16) for kk in k]
    decay_tot = [jnp.exp(gt) for gt in g_tot]

    kt_all = [_dot_tn(t, id4) for t in kb]
    kt = [kt_all[ci][kcols(h)] for ci, h in units]
    sx = [_dot(q_lvl[3][ci][:, kcols(h)], t.astype(BF16)) * mx_ref[3] for (ci, h), t in zip(units, kt)]
    for l in range(3):
        e_all = [_dot_tn_const(g, kmt_ref[l], 2) for g in gk]
        e = [e_all[ci][kcols(h)] for ci, h in units]
        kt_l = [(t * jnp.exp(e_)).astype(BF16) for t, e_ in zip(kt, e)]
        sx = [s_ + _dot(q_lvl[l][ci][:, kcols(h)], t) * mx_ref[l] for (ci, h), s_, t in zip(units, sx, kt_l)]
    vh = [vb[ci][:, vcols(h)] for ci, h in units]
    o_intra = [_dot(s_.astype(BF16), jnp.concatenate([v_] * GLA_SUB, axis=0)) for s_, v_ in zip(sx, vh)]
    kv = [_dot_tn(v_, k_out[ci][:, kcols(h)]) for (ci, h), v_ in zip(units, vh)]

    st = [st_ref[h] for h in range(N_HEAD)]
    for ci in range(CPB):
        c = order[ci]
        for h in range(N_HEAD):
            n = ci * N_HEAD + h
            o = o_intra[n] + _dot_nt(q_in[ci][:, kcols(h)], st[h].astype(BF16))
            st[h] = st[h] * decay_tot[ci][:, kcols(h)] + kv[n]
            if finalize:
                o = _finalize(o, oprev_ref, gate_ref, nw_ref, c * CHUNK, h * GLA_DV, GLA_DV)
            o_ref[rows(c), vcols(h)] = o.astype(o_ref.dtype)
    for h in range(N_HEAD):
        st_ref[h] = st[h]


def _gla(p, p_small, w2, b2, norm_w, n_seq, n_lat_blk):
    out = None
    hw = N_HEAD * GLA_DK
    for rev in (True, False):
        finalize = not rev
        d = 1 if rev else 0
        lr0 = SM_LR + d * GLA_RANK
        w2p = jnp.zeros((V7X_LANES, hw), F32).at[lr0:lr0 + GLA_RANK].set(w2[d]).astype(BF16)
        rows = _scan_rows(n_seq, n_lat_blk, rev)
        extra = [(p_small, pl.BlockSpec((SCAN_BLK, V7X_LANES), lambda b_, j, rows=rows: (rows(b_, j), 0)))]
        kern = functools.partial(_gla_kernel, rev=rev, finalize=finalize)
        out = _scan_call(kern, "gla_bwd" if rev else "gla_fwd", n_seq, n_lat_blk, rev, finalize,
                         [(p, CB_GLA_QK), (p, CB_GLA_V)], [], [w2p, b2[d].reshape(1, hw), *_gla_consts(rev)],
                         extra, out, (p, CB_GLA_GATE), norm_w,
                         [pltpu.VMEM((N_HEAD, GLA_DV, GLA_DK), F32)])
    return out


def _hyena_filters(L, w1, b1, w2, b2, w3, freq):
    t = jnp.linspace(0.0, 1.0, L, dtype=F32)[:, None]
    w = 2.0 * math.pi * jnp.arange(L, dtype=F32) / L
    f = jnp.linspace(1e-4, HY_BANDS - 1, HY_BANDS, dtype=F32)
    ang = w[:, None] * f[None, :]
    z = jnp.concatenate([t, jnp.cos(ang), -jnp.sin(ang)], axis=-1)
    hdn = jnp.sin(freq[0] * (z @ w1 + b1))
    for i in range(HY_INNER):
        hdn = jnp.sin(freq[i + 1] * (hdn @ w2[i] + b2[i]))
    h = (hdn @ w3).reshape(L, HY_ORDER, 2, HY_W)
    deltas = jnp.abs(jnp.linspace(HY_MIN_DECAY, HY_MAX_DECAY, HY_W, dtype=F32))
    h = h * jnp.exp(-t[:, :, None, None] * deltas)
    hf, hb = h[:, :, 0], h[:, :, 1]
    hb = hb.at[0].set(0.0)
    scale = lax.rsqrt(jnp.sum(hf * hf, axis=0) + jnp.sum(hb * hb, axis=0) + EPS)
    return (hf * scale).reshape(L, -1), (hb * scale).reshape(L, -1)


def _dft_mats(L):
    k = jnp.arange(L, dtype=jnp.int32)[:, None]
    n = jnp.arange(L, dtype=jnp.int32)[None, :]
    ang = (((2 * k + 1) * n) % (4 * L)).astype(F32) * (math.pi / (2 * L))
    return jnp.stack([jnp.cos(ang), jnp.sin(ang)]).astype(BF16)


def _hy_spectra(hf, hb, dft):
    def mm2(m, g):
        n = g.shape[1]
        out = _matmul(m, jnp.concatenate(_split_bf16(g, 2), axis=1))
        return out[:, :n] + out[:, n:]

    return mm2(dft[0], hf + hb), mm2(dft[1], hb - hf)


def _hy_kernel(z_ref, x_ref, skip_ref, cs_ref, kr_ref, ki_ref, *rest, last, n_k, slab, inv_scale):
    if last:
        gate_ref, o_ref, acc_ref = rest
    else:
        o_ref, acc_ref = rest
    kt = pl.program_id(2)

    @pl.when(kt == 0)
    def _():
        acc_ref[...] = jnp.zeros_like(acc_ref)

    tk, L = cs_ref.shape[1], cs_ref.shape[2]
    cs = cs_ref[...].reshape(2 * tk, L)
    zf = _dot(cs, z_ref[...])
    zr, sz = zf[:tk], zf[tk:]
    kr, ki = kr_ref[...], ki_ref[...]
    pp = jnp.concatenate([zr * kr + sz * ki, sz * kr - zr * ki], axis=0).astype(BF16)
    for r0 in range(0, L, slab):
        acc_ref[r0:r0 + slab, :] += _dot_tn(cs[:, r0:r0 + slab], pp)

    @pl.when(kt == n_k - 1)
    def _():
        for r0 in range(0, acc_ref.shape[0], slab):
            conv = acc_ref[r0:r0 + slab, :] * inv_scale
            z = z_ref[r0:r0 + slab, :].astype(F32)
            y = x_ref[r0:r0 + slab, :].astype(F32) * (conv + skip_ref[...] * z)
            if last:
                y = y * _silu(gate_ref[r0:r0 + slab, :].astype(F32))
            o_ref[r0:r0 + slab, :] = y.astype(o_ref.dtype)


HY_WT = HY_W
HY_VMEM_LIMIT_BYTES = 61 * 1024 * 1024


def _hy_order(z_src, x_src, gate_src, skip, dft, kr, ki, order, n_seq, L):
    last = gate_src is not None
    tk = min(512, L)
    n_k = L // tk
    slab = min(512, L)
    n_wt = HY_W // HY_WT

    def rowspec(src):
        _, r0, c0 = src
        return pl.BlockSpec((L, HY_WT), lambda b, w, k: (r0 + b, c0 + w), pipeline_mode=pl.Buffered(1))

    srcs = [z_src, x_src] + ([gate_src] if last else [])
    twid = pl.BlockSpec((2, tk, L), lambda b, w, k: (0, k, 0))
    spec = pl.BlockSpec((tk, HY_WT), lambda b, w, k: (k, order * n_wt + w))
    kern = functools.partial(_hy_kernel, last=last, n_k=n_k, slab=slab, inv_scale=1.0 / L)
    args = [z_src[0], x_src[0], skip[order].reshape(1, HY_W), dft, kr, ki]
    specs = [rowspec(z_src), rowspec(x_src), pl.BlockSpec((1, HY_WT), lambda b, w, k: (0, w)),
             twid, spec, spec]
    if last:
        args.append(gate_src[0])
        specs.append(rowspec(gate_src))
    return pl.pallas_call(
        kern,
        grid=(n_seq, n_wt, n_k),
        in_specs=specs,
        out_specs=pl.BlockSpec((L, HY_WT), lambda b, w, k: (b, w)),
        out_shape=jax.ShapeDtypeStruct((n_seq * L, HY_W), BF16),
        scratch_shapes=[pltpu.VMEM((L, HY_WT), F32)],
        compiler_params=pltpu.CompilerParams(dimension_semantics=("parallel", "parallel", "arbitrary"),
                                             vmem_limit_bytes=HY_VMEM_LIMIT_BYTES),
        name="hyena_order%d_L%d" % (order, L),
    )(*args)


def _hyena(p, conv_w, conv_b, w1, b1, w2, b2, w3, freq, skip, dfts, n_seq, seq_len):
    zcs = _conv_prep(p, CB_HY_V, conv_w, conv_b, n_seq, seq_len, False)
    n_wt = HY_W // HY_WT
    outs = []
    for zc, L, row0 in zip(zcs, (seq_len, CTX_LEN), (0, n_seq * seq_len // CTX_LEN)):
        dft = dfts[L]
        kr, ki = _hy_spectra(*_hyena_filters(L, w1, b1, w2, b2, w3, freq), dft)
        y1 = _hy_order((zc, 0, 0), (zc, 0, n_wt), None, skip, dft, kr, ki, 0, n_seq, L)
        outs.append(_hy_order((y1, 0, 0), (zc, 0, 2 * n_wt), (p, row0, CB_HY_GATE * n_wt),
                              skip, dft, kr, ki, 1, n_seq, L))
    return jnp.concatenate(outs, axis=0)


def _layout_w_in(w_in):
    sizes = [3 * HY_W, HY_W, 3 * BR_W, 2 * N_HEAD, 2 * N_HEAD, BR_W, 2 * N_HEAD * RET_DK, BR_W, BR_W,
             2 * N_HEAD * GLA_DK, BR_W, 2 * GLA_RANK, BR_W]
    offs = np.concatenate([[0], np.cumsum(sizes)])
    (hy_proj, hy_gate, gdn_qkv, gdn_a, gdn_b, gdn_gate, ret_qk, ret_v, ret_gate,
     gla_qk, gla_v, gla_lr, gla_gate) = [w_in[:, offs[n]:offs[n + 1]] for n in range(len(sizes))]
    D = w_in.shape[0]
    half = RET_DK // 2
    ret_rot = ret_qk.reshape(D, 2 * N_HEAD, 2, half)[:, :, ::-1].reshape(D, -1)
    main = jnp.concatenate([hy_proj, hy_gate, gdn_qkv, gdn_gate, ret_qk, ret_rot, ret_v, ret_gate,
                            gla_qk, gla_v, gla_gate], axis=1)
    small = jnp.concatenate([gdn_a, gdn_b, gla_lr], axis=1)
    small = jnp.pad(small, ((0, 0), (0, V7X_LANES - small.shape[1])))
    return main, small


def _layer(x, mod, dims, rope, dfts, norm_w, w_in, hy_conv_w, hy_conv_b, hy_w1, hy_b1, hy_w2, hy_b2, hy_w3,
           hy_freq, hy_skip, gdn_conv_w, gdn_a_log, gdn_dt_bias, gdn_norm, ret_norm, gla_w2, gla_b2,
           gla_norm, w_branch, w_merge, b_merge, w_out):
    n_seq, seq_len = dims
    n_lat_rows = n_seq * seq_len
    n_lat_blk = seq_len // SCAN_BLK
    shift, scale, gate = mod
    w_main, w_small = _layout_w_in(w_in.astype(BF16))
    p, h = _inproj(x, norm_w, scale, shift, w_main, n_lat_rows, seq_len)
    p_small = _matmul(h, w_small)

    y_hy = _hyena(p, hy_conv_w, hy_conv_b, hy_w1, hy_b1, hy_w2, hy_b2, hy_w3, hy_freq, hy_skip, dfts,
                  n_seq, seq_len)
    y_gdn = _gdn(p, p_small, gdn_conv_w, gdn_a_log, gdn_dt_bias, gdn_norm, n_seq, n_lat_blk)
    y_ret = _retention(p, rope[0], rope[1], ret_norm, n_seq, n_lat_blk)
    y_gla = _gla(p, p_small, gla_w2, gla_b2, gla_norm, n_seq, n_lat_blk)

    merged = _merge(h, (y_hy, y_gdn, y_ret, y_gla), w_merge.astype(BF16), b_merge, w_branch.astype(BF16))
    return _outproj(merged, w_out.astype(BF16), x, gate, n_lat_rows, seq_len)


def kernel(x, c, ctx, c_ctx, norm_w, ada_w, ada_b, w_in, hy_conv_w, hy_conv_b, hy_w1, hy_b1,
           hy_w2, hy_b2, hy_w3, hy_freq, hy_skip, gdn_conv_w, gdn_a_log, gdn_dt_bias,
           gdn_norm, ret_norm, gla_w2, gla_b2, gla_norm, w_branch, w_merge, b_merge, w_out,
           final_norm):
    B, S, D = x.shape
    layer_params = (norm_w, w_in, hy_conv_w, hy_conv_b, hy_w1, hy_b1, hy_w2, hy_b2, hy_w3, hy_freq,
                    hy_skip, gdn_conv_w, gdn_a_log, gdn_dt_bias, gdn_norm, ret_norm, gla_w2, gla_b2,
                    gla_norm, w_branch, w_merge, b_merge, w_out)
    xt = jnp.concatenate([x.reshape(-1, D), ctx.reshape(-1, D)], axis=0)
    cond = _silu(jnp.concatenate([c_ctx[None], c], axis=0))
    cond = jnp.pad(cond, ((0, -(1 + B) % 8), (0, 0))).astype(BF16)
    rope = _rope_tables(S)
    dfts = {L: _dft_mats(L) for L in (S, CTX_LEN)}
    for l in range(DEPTH):
        m = (_matmul(cond, ada_w[l].astype(BF16)) + ada_b[l])[:1 + B]
        mod = tuple(t.reshape(1 + B, 1, D) for t in jnp.split(m, 3, axis=-1))
        xt = _layer(xt, mod, (B, S), rope, dfts, *[p[l] for p in layer_params])
    return _final_norm(xt, final_norm, B * S).reshape(B, S, D)
```
